```python
import math, functools
import jax, jax.numpy as jnp
from jax import lax
import numpy as np

D_MODEL = 4096
BATCH = 1
SEQ = 8192
DEPTH = 1
DEC_BATCH = 32
DEC_SEQ = 8
PAST_LEN = 8192
PAGE_SIZE = 128

DA_HEADS = 8
DA_DH = 128
DA_WIDTH = DA_HEADS * 2 * DA_DH
RW_HEAD = 64
RW_WIDTH = D_MODEL - DA_WIDTH
RW_HEADS = RW_WIDTH // RW_HEAD
DECAY_LORA = 96
AAA_LORA = 96
GATE_LORA = 256
RW_PROJ = 3 * RW_WIDTH + DECAY_LORA + AAA_LORA + GATE_LORA
IN_COLS = 3 * DA_WIDTH + RW_PROJ
MEM_LEN = 256
MEM_HEADS = 4
MEM_DH = 128
MEM_WIDTH = MEM_HEADS * MEM_DH
D_FF = -(-8 * D_MODEL // (3 * 256)) * 256
Q_BLOCK = 128
LN_EPS = 1e-5
SUBLN_EPS = 1e-5
RW_GN_EPS = 64e-5
ALPHA = (2.0 * DEPTH) ** 0.25
BETA = (8.0 * DEPTH) ** -0.25

kernel_name = "hymba_diffattn_rwkv7_deepnorm_step"

F32 = jnp.float32


def _layernorm(x, g, b):
    xf = x.astype(F32)
    mu = xf.mean(-1, keepdims=True)
    var = jnp.square(xf - mu).mean(-1, keepdims=True)
    return ((xf - mu) * lax.rsqrt(var + LN_EPS) * g + b).astype(x.dtype)


def _alibi(qpos, kpos):
    slopes = 2.0 ** (-8.0 * jnp.arange(1, DA_HEADS + 1, dtype=F32) / DA_HEADS)
    dist = (qpos[:, None] - kpos[None, :]).astype(F32)
    bias = -slopes[:, None, None] * dist[None]
    return jnp.where(dist[None] >= 0, bias, -jnp.inf)


def _scores(q, k):
    return jnp.einsum('bqhcd,bshcd->bchqs', q, k) * (DA_DH ** -0.5)


def _diff_attn_prompt(q, k, v, lam):
    B, T = q.shape[:2]
    nb = T // Q_BLOCK
    kf = k.astype(F32)
    vf = v.astype(F32)
    qb = jnp.moveaxis(q.astype(F32).reshape(B, nb, Q_BLOCK, DA_HEADS, 2, DA_DH), 1, 0)
    kpos = jnp.arange(T)

    def block(args):
        q_blk, i = args
        qpos = i * Q_BLOCK + jnp.arange(Q_BLOCK)
        pr = jax.nn.softmax(_scores(q_blk, kf) + _alibi(qpos, kpos)[None, None], axis=-1)
        a = pr[:, 0] - lam * pr[:, 1]
        return jnp.einsum('bhqs,bshe->bqhe', a, vf)

    o = lax.map(block, (qb, jnp.arange(nb)))
    return jnp.moveaxis(o, 0, 1).reshape(B, T, DA_HEADS, 2 * DA_DH)


def _online_update(carry, q, k_blk, v_blk, bias):
    m, l, acc = carry
    s = _scores(q, k_blk) + bias[None, None]
    m_new = jnp.maximum(m, s.max(-1))
    corr = jnp.exp(m - m_new)
    p = jnp.exp(s - m_new[..., None])
    acc = acc * corr[..., None] + jnp.einsum('bchqs,bshe->bchqe', p, v_blk)
    return (m_new, l * corr + p.sum(-1), acc)


def _diff_attn_sample(q, k, v, lam, cache_k, cache_v, page_table):
    DB, S = q.shape[:2]
    n_pages = page_table.shape[1]
    qpos = n_pages * PAGE_SIZE + jnp.arange(S)
    qf = q.astype(F32)
    init = (jnp.full((DB, 2, DA_HEADS, S), -jnp.inf, F32),
            jnp.zeros((DB, 2, DA_HEADS, S), F32),
            jnp.zeros((DB, 2, DA_HEADS, S, 2 * DA_DH), F32))

    def page_step(carry, xs):
        phys, lp = xs
        kpos = lp * PAGE_SIZE + jnp.arange(PAGE_SIZE)
        k_blk = cache_k[phys].astype(F32).reshape(DB, PAGE_SIZE, DA_HEADS, 2, DA_DH)
        v_blk = cache_v[phys].astype(F32)
        return _online_update(carry, qf, k_blk, v_blk, _alibi(qpos, kpos)), None

    carry, _ = lax.scan(page_step, init, (page_table.T, jnp.arange(n_pages)))
    m, l, acc = _online_update(carry, qf, k.astype(F32), v.astype(F32), _alibi(qpos, qpos))
    o = acc / l[..., None]
    out = o[:, 0] - lam * o[:, 1]
    return jnp.transpose(out, (0, 2, 1, 3))


def _wkv_scan(S0, r, w, k, v, kk, a):
    def step(S, xs):
        r_t, w_t, k_t, v_t, kk_t, a_t = xs
        sa = jnp.einsum('bhij,bhj->bhi', S, -kk_t)
        S = (S * w_t[:, :, None, :] + sa[..., None] * (kk_t * a_t)[:, :, None, :]
             + v_t[..., None] * k_t[:, :, None, :])
        return S, jnp.einsum('bhij,bhj->bhi', S, r_t)
    xs = tuple(jnp.moveaxis(t.astype(F32), 1, 0) for t in (r, w, k, v, kk, a))
    S, o = lax.scan(step, S0.astype(F32), xs)
    return S, jnp.moveaxis(o, 0, 1)


def _rwkv_mix(feat, prev_row, S0, p):
    B, T = feat.shape[:2]
    shifted = jnp.concatenate([prev_row[:, None].astype(feat.dtype), feat[:, :-1]], axis=1)
    xm = feat + (shifted - feat) * p['mu_shift']
    c = RW_WIDTH
    r, k, v, dw, da, dg = jnp.split(
        xm, [c, 2 * c, 3 * c, 3 * c + DECAY_LORA, 3 * c + DECAY_LORA + AAA_LORA], axis=-1)
    w_log = -jax.nn.softplus(-(p['w0'] + jnp.tanh(dw) @ p['w2']).astype(F32)) - 0.5
    w = jnp.exp(-jnp.exp(w_log))
    a = jax.nn.sigmoid((p['a0'] + da @ p['a2']).astype(F32))
    g = jax.nn.sigmoid(dg) @ p['g2']
    hs = lambda t: t.reshape(B, T, RW_HEADS, RW_HEAD)
    kk = hs((k * p['k_k']).astype(F32))
    kk = kk / jnp.maximum(jnp.sqrt(jnp.sum(kk * kk, -1, keepdims=True)), 1e-12)
    k = k * (1.0 + (a - 1.0) * p['k_a'])
    rh, kh, vh = hs(r).astype(F32), hs(k).astype(F32), hs(v).astype(F32)
    S_T, o = _wkv_scan(S0, rh, hs(w), kh, vh, kk, hs(a))
    mu = o.mean(-1, keepdims=True)
    var = jnp.square(o - mu).mean(-1, keepdims=True)
    o = ((o - mu) * lax.rsqrt(var + RW_GN_EPS)).reshape(B, T, c) * p['lnx_g'] + p['lnx_b']
    o = o + (jnp.sum(rh * kh * p['r_k'], -1, keepdims=True) * vh).reshape(B, T, c)
    return o * g, S_T, feat[:, -1]


def _mem_kv(mem, wk, wv):
    B = mem.shape[0]
    return ((mem @ wk).reshape(B, -1, MEM_HEADS, MEM_DH),
            (mem @ wv).reshape(B, -1, MEM_HEADS, MEM_DH))


def _mem_attn(x, mk, mv, wq, wo):
    B, T = x.shape[:2]
    q = (x @ wq).reshape(B, T, MEM_HEADS, MEM_DH)
    s = jnp.einsum('bthd,bmhd->bhtm', q.astype(F32), mk.astype(F32)) * (MEM_DH ** -0.5)
    pr = jax.nn.softmax(s, axis=-1)
    o = jnp.einsum('bhtm,bmhd->bthd', pr, mv.astype(F32)).reshape(B, T, MEM_WIDTH)
    return o.astype(x.dtype) @ wo


def _layer(x, attn_fn, wkv0, shift0, mem_k, mem_v, p, lam_init):
    B, T = x.shape[:2]
    proj = x @ p['w_in']
    qa, ka, va, feat = jnp.split(proj, [DA_WIDTH, 2 * DA_WIDTH, 3 * DA_WIDTH], axis=-1)
    q = qa.reshape(B, T, DA_HEADS, 2, DA_DH)
    k = ka.reshape(B, T, DA_HEADS, 2, DA_DH)
    v = va.reshape(B, T, DA_HEADS, 2 * DA_DH)
    lam = (jnp.exp(jnp.sum(p['lam_q1'].astype(F32) * p['lam_k1'].astype(F32)))
           - jnp.exp(jnp.sum(p['lam_q2'].astype(F32) * p['lam_k2'].astype(F32))) + lam_init)
    o_a = attn_fn(q, k, v, lam)
    o_a = o_a * lax.rsqrt(jnp.mean(o_a * o_a, -1, keepdims=True) + SUBLN_EPS) * p['subln_g']
    o_a = (o_a * (1.0 - lam_init)).reshape(B, T, DA_WIDTH)
    o_r, wkv_T, last = _rwkv_mix(feat, shift0, wkv0, p)
    mix = jnp.concatenate([o_a.astype(x.dtype), o_r.astype(x.dtype)], axis=-1) @ p['w_out']
    x = _layernorm(ALPHA * x + mix, p['ln1_g'], p['ln1_b'])
    x = _layernorm(ALPHA * x + _mem_attn(x, mem_k, mem_v, p['wq_m'], p['wo_m']), p['ln2_g'], p['ln2_b'])
    ff = (jax.nn.silu(x @ p['w_gate']) * (x @ p['w_up'])) @ p['w_down']
    x = _layernorm(ALPHA * x + ff, p['ln3_g'], p['ln3_b'])
    return x, k.reshape(B, T, DA_HEADS, 2 * DA_DH), v, wkv_T, last


def setup_inputs(seed: int = 0) -> dict:
    key = jax.random.key(seed)
    keys = list(jax.random.split(key, 64))

    def nrm(shape, scale):
        return jax.random.normal(keys.pop(), shape, F32) * scale

    def unif(shape, lo, hi):
        return jax.random.uniform(keys.pop(), shape, F32, minval=lo, maxval=hi)

    L = DEPTH
    n_pages = PAST_LEN // PAGE_SIZE
    used = DEC_BATCH * n_pages
    n_phys = used + max(1, used // 4)
    page_table = jax.random.permutation(keys.pop(), n_phys)[:used].reshape(DEC_BATCH, n_pages).astype(jnp.int32)
    return {
        "x_prompt": nrm((BATCH, SEQ, D_MODEL), 1.0),
        "x_sample": nrm((DEC_BATCH, DEC_SEQ, D_MODEL), 1.0),
        "cache_k": nrm((L, n_phys, PAGE_SIZE, DA_HEADS, 2 * DA_DH), 1.0),
        "cache_v": nrm((L, n_phys, PAGE_SIZE, DA_HEADS, 2 * DA_DH), 1.0),
        "cache_mem_k": nrm((L, DEC_BATCH, MEM_LEN, MEM_HEADS, MEM_DH), 1.0),
        "cache_mem_v": nrm((L, DEC_BATCH, MEM_LEN, MEM_HEADS, MEM_DH), 1.0),
        "state_wkv": nrm((L, DEC_BATCH, RW_HEADS, RW_HEAD, RW_HEAD), 0.3),
        "state_shift": nrm((L, DEC_BATCH, RW_PROJ), 1.0),
        "page_table": page_table,
        "mem_prompt": nrm((BATCH, MEM_LEN, D_MODEL), 1.0),
        "w_in": nrm((L, D_MODEL, IN_COLS), D_MODEL ** -0.5),
        "lam_q1": nrm((L, DA_DH), 0.1),
        "lam_k1": nrm((L, DA_DH), 0.1),
        "lam_q2": nrm((L, DA_DH), 0.1),
        "lam_k2": nrm((L, DA_DH), 0.1),
        "subln_g": 1.0 + nrm((L, 2 * DA_DH), 0.05),
        "mu_shift": unif((L, RW_PROJ), 0.05, 0.95),
        "w0": unif((L, RW_WIDTH), -6.0, -1.0),
        "w2": nrm((L, DECAY_LORA, RW_WIDTH), 0.5 * DECAY_LORA ** -0.5),
        "a0": nrm((L, RW_WIDTH), 0.1),
        "a2": nrm((L, AAA_LORA, RW_WIDTH), 0.5 * AAA_LORA ** -0.5),
        "g2": nrm((L, GATE_LORA, RW_WIDTH), GATE_LORA ** -0.5),
        "k_k": 0.85 + nrm((L, RW_WIDTH), 0.05),
        "k_a": 1.0 + nrm((L, RW_WIDTH), 0.05),
        "r_k": nrm((L, RW_HEADS, RW_HEAD), 0.1),
        "lnx_g": 1.0 + nrm((L, RW_WIDTH), 0.05),
        "lnx_b": nrm((L, RW_WIDTH), 0.01),
        "w_out": nrm((L, D_MODEL, D_MODEL), BETA * D_MODEL ** -0.5),
        "ln1_g": 1.0 + nrm((L, D_MODEL), 0.05),
        "ln1_b": nrm((L, D_MODEL), 0.01),
        "wq_m": nrm((L, D_MODEL, MEM_WIDTH), D_MODEL ** -0.5),
        "wk_m": nrm((L, D_MODEL, MEM_WIDTH), D_MODEL ** -0.5),
        "wv_m": nrm((L, D_MODEL, MEM_WIDTH), D_MODEL ** -0.5),
        "wo_m": nrm((L, MEM_WIDTH, D_MODEL), BETA * MEM_WIDTH ** -0.5),
        "ln2_g": 1.0 + nrm((L, D_MODEL), 0.05),
        "ln2_b": nrm((L, D_MODEL), 0.01),
        "w_gate": nrm((L, D_MODEL, D_FF), D_MODEL ** -0.5),
        "w_up": nrm((L, D_MODEL, D_FF), D_MODEL ** -0.5),
        "w_down": nrm((L, D_FF, D_MODEL), BETA * D_FF ** -0.5),
        "ln3_g": 1.0 + nrm((L, D_MODEL), 0.05),
        "ln3_b": nrm((L, D_MODEL), 0.01),
    }


def reference(x_prompt, x_sample, cache_k, cache_v, cache_mem_k, cache_mem_v, state_wkv, state_shift,
              page_table, mem_prompt, w_in, lam_q1, lam_k1, lam_q2, lam_k2, subln_g, mu_shift, w0, w2,
              a0, a2, g2, k_k, k_a, r_k, lnx_g, lnx_b, w_out, ln1_g, ln1_b, wq_m, wk_m, wv_m, wo_m,
              ln2_g, ln2_b, w_gate, w_up, w_down, ln3_g, ln3_b):
    yp, ys = x_prompt, x_sample
    kp_l, vp_l, wkvp_l, shp_l, mkp_l, mvp_l = [], [], [], [], [], []
    ks_l, vs_l, wkvs_l, shs_l = [], [], [], []
    for l in range(DEPTH):
        p = dict(w_in=w_in[l], lam_q1=lam_q1[l], lam_k1=lam_k1[l], lam_q2=lam_q2[l], lam_k2=lam_k2[l],
                 subln_g=subln_g[l], mu_shift=mu_shift[l], w0=w0[l], w2=w2[l], a0=a0[l], a2=a2[l],
                 g2=g2[l], k_k=k_k[l], k_a=k_a[l], r_k=r_k[l], lnx_g=lnx_g[l], lnx_b=lnx_b[l],
                 w_out=w_out[l], ln1_g=ln1_g[l], ln1_b=ln1_b[l], wq_m=wq_m[l], wo_m=wo_m[l],
                 ln2_g=ln2_g[l], ln2_b=ln2_b[l], w_gate=w_gate[l], w_up=w_up[l], w_down=w_down[l],
                 ln3_g=ln3_g[l], ln3_b=ln3_b[l])
        lam_init = 0.8 - 0.6 * math.exp(-0.3 * l)
        mk_p, mv_p = _mem_kv(mem_prompt, wk_m[l], wv_m[l])
        wkv0 = jnp.zeros((yp.shape[0], RW_HEADS, RW_HEAD, RW_HEAD), F32)
        sh0 = jnp.zeros((yp.shape[0], RW_PROJ), yp.dtype)
        yp, kp, vp, wkvp, shp = _layer(yp, _diff_attn_prompt, wkv0, sh0, mk_p, mv_p, p, lam_init)
        attn_s = functools.partial(_diff_attn_sample, cache_k=cache_k[l], cache_v=cache_v[l],
                                   page_table=page_table)
        ys, ks, vs, wkvs, shs = _layer(ys, attn_s, state_wkv[l], state_shift[l],
                                       cache_mem_k[l], cache_mem_v[l], p, lam_init)
        kp_l.append(kp); vp_l.append(vp); wkvp_l.append(wkvp); shp_l.append(shp)
        mkp_l.append(mk_p); mvp_l.append(mv_p)
        ks_l.append(ks); vs_l.append(vs); wkvs_l.append(wkvs); shs_l.append(shs)
    return (yp, ys, jnp.stack(kp_l), jnp.stack(vp_l), jnp.stack(wkvp_l), jnp.stack(shp_l),
            jnp.stack(mkp_l), jnp.stack(mvp_l), jnp.stack(ks_l), jnp.stack(vs_l), jnp.stack(wkvs_l),
            jnp.stack(shs_l))
```

```python
import functools
import math

import jax
import jax.numpy as jnp
import numpy as np
from jax import lax
from jax.experimental import pallas as pl
from jax.experimental.pallas import tpu as pltpu

F32 = jnp.float32
BF16 = jnp.bfloat16

D_MODEL = 4096
DA_HEADS = 8
DA_DH = 128
DA_HV = 2 * DA_DH
DA_WIDTH = DA_HEADS * DA_HV
RW_HEAD = 64
RW_WIDTH = 2048
RW_HEADS = RW_WIDTH // RW_HEAD
DECAY_LORA = 96
AAA_LORA = 96
GATE_LORA = 256
RW_PROJ = 3 * RW_WIDTH + DECAY_LORA + AAA_LORA + GATE_LORA
LORA_PAD = 128
LORA_BLOCK = 2 * LORA_PAD + GATE_LORA
FEAT_PAD = 3 * RW_WIDTH + LORA_BLOCK
PROJ_PAD = 3 * DA_WIDTH + FEAT_PAD
MEM_LEN = 256
MEM_HEADS = 4
MEM_DH = 128
MEM_WIDTH = MEM_HEADS * MEM_DH
PAGE_SIZE = 128
DEPTH = 1
LN_EPS = 1e-5
SUBLN_EPS = 1e-5
RW_GN_EPS = 64e-5
ALPHA = (2.0 * DEPTH) ** 0.25

LANES = 128
VMEM_LIMIT = 56 * 1024 * 1024
NEG_INF = float("-inf")

_NT = (((1,), (1,)), ((), ()))


def _cparams(*sem):
    return pltpu.CompilerParams(dimension_semantics=sem, vmem_limit_bytes=VMEM_LIMIT)


def _layernorm(x, g, b):
    mu = jnp.mean(x, axis=-1, keepdims=True)
    d = x - mu
    var = jnp.mean(d * d, axis=-1, keepdims=True)
    return d * lax.rsqrt(var + LN_EPS) * g + b


def _sigmoid(x):
    return 1.0 / (1.0 + jnp.exp(-x))


def _split2(x):
    hi = x.astype(BF16)
    lo = (x - hi.astype(F32)).astype(BF16)
    return hi, lo


def _segsum(x, jj_ref):
    hi, lo = _split2(x)
    return jnp.dot(jnp.concatenate([hi, lo], axis=1), jj_ref[...], preferred_element_type=F32)


def _seg_ones(parts):
    r = np.arange(parts * LANES)[:, None] % LANES
    c = np.arange(LANES)[None, :]
    return jnp.asarray((r // RW_HEAD) == (c // RW_HEAD), dtype=BF16)


def _diag_mask():
    r = np.arange(RW_HEAD)[:, None]
    c = np.arange(LANES)[None, :]
    return jnp.asarray((c % RW_HEAD) == r, dtype=F32)


def _mm_kernel(a_ref, w_ref, o_ref):
    o_ref[...] = jnp.dot(a_ref[...], w_ref[...], preferred_element_type=F32)


def _matmul(a, w, tm, tn):
    m, k = a.shape
    n = w.shape[1]
    assert m % tm == 0 and n % tn == 0
    return pl.pallas_call(
        _mm_kernel,
        grid=(m // tm, n // tn),
        in_specs=[pl.BlockSpec((tm, k), lambda i, j: (i, 0)),
                  pl.BlockSpec((k, tn), lambda i, j: (0, j))],
        out_specs=pl.BlockSpec((tm, tn), lambda i, j: (i, j)),
        out_shape=jax.ShapeDtypeStruct((m, n), F32),
        compiler_params=_cparams("parallel", "arbitrary"),
        name="matmul",
    )(a, w)


def _lambda(lamv_ref, lam_init):
    lv = lamv_ref[...]
    s1 = jnp.sum(lv[0:1] * lv[1:2], axis=-1, keepdims=True)
    s2 = jnp.sum(lv[2:3] * lv[3:4], axis=-1, keepdims=True)
    return jnp.exp(s1) - jnp.exp(s2) + lam_init


def _online_softmax_step(s, v, m_ref, l_ref, acc_ref, idx):
    m_old = m_ref[idx]
    m_new = jnp.maximum(m_old, jnp.max(s, axis=-1, keepdims=True))
    corr = jnp.exp(m_old - m_new)
    p = jnp.exp(s - m_new)
    l_ref[idx] = l_ref[idx] * corr + jnp.sum(p, axis=-1, keepdims=True)
    acc_ref[idx] = acc_ref[idx] * corr + jnp.dot(p.astype(BF16), v, preferred_element_type=F32)
    m_ref[idx] = m_new


def _diff_combine(acc_ref, l_ref, i1, i2, lam, g, lam_init):
    o = acc_ref[i1] / l_ref[i1] - lam * (acc_ref[i2] / l_ref[i2])
    o = o * lax.rsqrt(jnp.mean(o * o, axis=-1, keepdims=True) + SUBLN_EPS) * g
    return o * (1.0 - lam_init)


def _pattn_kernel(qi_ref, ki_ref, slopes_ref, lamv_ref, g_ref, q_ref, k_ref, v_ref, o_ref,
                  qs_scr, m_scr, l_scr, acc_scr, *, lam_init, tq):
    h = pl.program_id(0)
    st = pl.program_id(1)
    qb = qi_ref[st]
    kb = ki_ref[st]
    slope = slopes_ref[h]

    @pl.when(kb == 0)
    def _init():
        qs_scr[...] = (q_ref[...] * (DA_DH ** -0.5)).astype(BF16)
        m_scr[...] = jnp.full(m_scr.shape, NEG_INF, F32)
        l_scr[...] = jnp.zeros(l_scr.shape, F32)
        acc_scr[...] = jnp.zeros(acc_scr.shape, F32)

    def _step(masked):
        k = k_ref[...].astype(BF16)
        v = v_ref[...].astype(BF16)
        col = lax.broadcasted_iota(jnp.int32, (1, tq), 1)
        bias = slope * ((kb - qb) * tq + col).astype(F32)
        if masked:
            row2 = lax.broadcasted_iota(jnp.int32, (tq, tq), 0)
            col2 = lax.broadcasted_iota(jnp.int32, (tq, tq), 1)
            future = col2 > row2
        for c in range(2):
            s = lax.dot_general(qs_scr[:, c * DA_DH:(c + 1) * DA_DH], k[:, c * DA_DH:(c + 1) * DA_DH],
                                _NT, preferred_element_type=F32) + bias
            if masked:
                s = jnp.where(future, NEG_INF, s)
            _online_softmax_step(s, v, m_scr, l_scr, acc_scr, c)

    @pl.when(kb < qb)
    def _off_diag():
        _step(False)

    @pl.when(kb == qb)
    def _diag():
        _step(True)
        lam = _lambda(lamv_ref, lam_init)
        o_ref[...] = _diff_combine(acc_scr, l_scr, 0, 1, lam, g_ref[...], lam_init).astype(o_ref.dtype)


def _prompt_attention(proj, lamv, subln_g, lam_init, tq):
    t = proj.shape[0]
    assert t % tq == 0
    nq = t // tq
    qi = np.concatenate([np.full(q + 1, q, np.int32) for q in range(nq)])
    ki = np.concatenate([np.arange(q + 1, dtype=np.int32) for q in range(nq)])
    slopes = jnp.asarray(2.0 ** (-8.0 * np.arange(1, DA_HEADS + 1) / DA_HEADS), F32)
    grid_spec = pltpu.PrefetchScalarGridSpec(
        num_scalar_prefetch=3,
        grid=(DA_HEADS, len(qi)),
        in_specs=[
            pl.BlockSpec((4, DA_DH), lambda h, s, qi, ki, sl: (0, 0)),
            pl.BlockSpec((1, DA_HV), lambda h, s, qi, ki, sl: (0, 0)),
            pl.BlockSpec((tq, DA_HV), lambda h, s, qi, ki, sl: (qi[s], h)),
            pl.BlockSpec((tq, DA_HV), lambda h, s, qi, ki, sl: (ki[s], DA_HEADS + h)),
            pl.BlockSpec((tq, DA_HV), lambda h, s, qi, ki, sl: (ki[s], 2 * DA_HEADS + h)),
        ],
        out_specs=pl.BlockSpec((tq, DA_HV), lambda h, s, qi, ki, sl: (qi[s], h)),
        scratch_shapes=[
            pltpu.VMEM((tq, DA_HV), BF16),
            pltpu.VMEM((2, tq, 1), F32),
            pltpu.VMEM((2, tq, 1), F32),
            pltpu.VMEM((2, tq, DA_HV), F32),
        ],
    )
    return pl.pallas_call(
        functools.partial(_pattn_kernel, lam_init=lam_init, tq=tq),
        grid_spec=grid_spec,
        out_shape=jax.ShapeDtypeStruct((t, DA_WIDTH), BF16),
        compiler_params=_cparams("parallel", "arbitrary"),
        name="prompt_attention",
    )(jnp.asarray(qi), jnp.asarray(ki), slopes, lamv, subln_g, proj, proj, proj)


def _sattn_kernel(pt_ref, lamv_ref, g_ref, q_ref, kn_ref, vn_ref, *rest, pages_per_step, lam_init):
    pp = pages_per_step
    kp_refs = rest[:pp]
    vp_refs = rest[pp:2 * pp]
    o_ref = rest[2 * pp]
    qs_scr, m_scr, l_scr, acc_scr = rest[2 * pp + 1:]
    st = pl.program_id(1)
    n_steps = pl.num_programs(1)
    n_tok = q_ref.shape[0]
    past_len = n_steps * pp * PAGE_SIZE

    @pl.when(st == 0)
    def _init():
        qs_scr[...] = (q_ref[...] * (DA_DH ** -0.5)).astype(BF16)
        m_scr[...] = jnp.full(m_scr.shape, NEG_INF, F32)
        l_scr[...] = jnp.zeros(l_scr.shape, F32)
        acc_scr[...] = jnp.zeros(acc_scr.shape, F32)

    def _block(k_ref, v_ref, kpos0, masked):
        k = k_ref[...].astype(BF16)
        v = v_ref[...].astype(BF16)
        key = lax.broadcasted_iota(jnp.int32, (1, PAGE_SIZE), 1)
        rel = (kpos0 - past_len + key).astype(F32)
        if masked:
            tok2 = lax.broadcasted_iota(jnp.int32, (n_tok, PAGE_SIZE), 0)
            key2 = lax.broadcasted_iota(jnp.int32, (n_tok, PAGE_SIZE), 1)
            future = key2 > tok2
        for h in range(DA_HEADS):
            bias = (2.0 ** (-8.0 * (h + 1) / DA_HEADS)) * rel
            vh = v[:, h * DA_HV:(h + 1) * DA_HV]
            for c in range(2):
                lo = h * DA_HV + c * DA_DH
                s = lax.dot_general(qs_scr[:, lo:lo + DA_DH], k[:, lo:lo + DA_DH], _NT,
                                    preferred_element_type=F32) + bias
                if masked:
                    s = jnp.where(future, NEG_INF, s)
                _online_softmax_step(s, vh, m_scr, l_scr, acc_scr, 2 * h + c)

    for p in range(pp):
        _block(kp_refs[p], vp_refs[p], (st * pp + p) * PAGE_SIZE, False)

    @pl.when(st == n_steps - 1)
    def _last():
        _block(kn_ref, vn_ref, past_len, True)
        lam = _lambda(lamv_ref, lam_init)
        g = g_ref[...]
        for h in range(DA_HEADS):
            o_ref[:, h * DA_HV:(h + 1) * DA_HV] = _diff_combine(
                acc_scr, l_scr, 2 * h, 2 * h + 1, lam, g, lam_init).astype(o_ref.dtype)


def _sample_attention(proj, cache_k, cache_v, page_table, lamv, subln_g, lam_init, n_tok, pages_per_step):
    b, n_pages = page_table.shape
    pp = pages_per_step
    assert n_pages % pp == 0
    n_phys = cache_k.shape[0]
    ck = cache_k.reshape(n_phys, PAGE_SIZE, DA_WIDTH)
    cv = cache_v.reshape(n_phys, PAGE_SIZE, DA_WIDTH)
    proj3 = proj.reshape(b, n_tok, PROJ_PAD)
    pad = ((0, 0), (0, PAGE_SIZE - n_tok), (0, 0))
    k_new = jnp.pad(proj3[:, :, DA_WIDTH:2 * DA_WIDTH], pad)
    v_new = jnp.pad(proj3[:, :, 2 * DA_WIDTH:3 * DA_WIDTH], pad)

    def page_spec(p):
        return pl.BlockSpec((None, PAGE_SIZE, DA_WIDTH),
                            lambda bi, s, pt, p=p: (pt[bi * n_pages + s * pp + p], 0, 0))

    grid_spec = pltpu.PrefetchScalarGridSpec(
        num_scalar_prefetch=1,
        grid=(b, n_pages // pp),
        in_specs=[
            pl.BlockSpec((4, DA_DH), lambda bi, s, pt: (0, 0)),
            pl.BlockSpec((1, DA_HV), lambda bi, s, pt: (0, 0)),
            pl.BlockSpec((None, n_tok, DA_WIDTH), lambda bi, s, pt: (bi, 0, 0)),
            pl.BlockSpec((None, PAGE_SIZE, DA_WIDTH), lambda bi, s, pt: (bi, 0, 0)),
            pl.BlockSpec((None, PAGE_SIZE, DA_WIDTH), lambda bi, s, pt: (bi, 0, 0)),
        ] + [page_spec(p) for p in range(pp)] + [page_spec(p) for p in range(pp)],
        out_specs=pl.BlockSpec((None, n_tok, DA_WIDTH), lambda bi, s, pt: (bi, 0, 0)),
        scratch_shapes=[
            pltpu.VMEM((n_tok, DA_WIDTH), BF16),
            pltpu.VMEM((2 * DA_HEADS, n_tok, 1), F32),
            pltpu.VMEM((2 * DA_HEADS, n_tok, 1), F32),
            pltpu.VMEM((2 * DA_HEADS, n_tok, DA_HV), F32),
        ],
    )
    return pl.pallas_call(
        functools.partial(_sattn_kernel, pages_per_step=pp, lam_init=lam_init),
        grid_spec=grid_spec,
        out_shape=jax.ShapeDtypeStruct((b, n_tok, DA_WIDTH), F32),
        compiler_params=_cparams("parallel", "arbitrary"),
        name="sample_attention",
    )(page_table.reshape(-1), lamv, subln_g, proj3, k_new, v_new, *([ck] * pp), *([cv] * pp))


def _rwkv_prep_kernel(fr_ref, fk_ref, fv_ref, fl_ref, pr_ref, pk_ref, pv_ref, pl_ref,
                      mur_ref, muk_ref, muv_ref, mul_ref, w0_ref, a0_ref, kkw_ref, kaw_ref,
                      w2_ref, a2_ref, g2_ref, jj_ref,
                      r_out, w_out, k_out, v_out, kk_out, b_out, g_out,
                      cr_scr, ck_scr, cv_scr, cl_scr):
    tb = pl.program_id(1)
    tm = fr_ref.shape[0]

    @pl.when(tb == 0)
    def _load_state():
        cr_scr[...] = pr_ref[...]
        ck_scr[...] = pk_ref[...]
        cv_scr[...] = pv_ref[...]
        cl_scr[...] = pl_ref[...]

    def mixed(f_ref, carry_scr, mu_ref):
        x = f_ref[...]
        row = lax.broadcasted_iota(jnp.int32, x.shape, 0)
        shifted = jnp.where(row == 0, carry_scr[...], pltpu.roll(x, 1, 0))
        carry_scr[...] = x[tm - 1:tm, :]
        return x + (shifted - x) * mu_ref[...]

    xr = mixed(fr_ref, cr_scr, mur_ref)
    xk = mixed(fk_ref, ck_scr, muk_ref)
    xv = mixed(fv_ref, cv_scr, muv_ref)
    xl = mixed(fl_ref, cl_scr, mul_ref)

    dw = jnp.tanh(xl[:, 0:LORA_PAD]).astype(BF16)
    da = xl[:, LORA_PAD:2 * LORA_PAD].astype(BF16)
    dg = _sigmoid(xl[:, 2 * LORA_PAD:]).astype(BF16)
    wl = w0_ref[...] + jnp.dot(dw, w2_ref[...], preferred_element_type=F32)
    neg = -wl
    softplus = jnp.maximum(neg, 0.0) + jnp.log(1.0 + jnp.exp(-jnp.abs(neg)))
    w = jnp.exp(-jnp.exp(-softplus - 0.5))
    a = _sigmoid(a0_ref[...] + jnp.dot(da, a2_ref[...], preferred_element_type=F32))
    g = jnp.dot(dg, g2_ref[...], preferred_element_type=F32)

    kk0 = xk * kkw_ref[...]
    sq = kk0 * kk0
    ss = jnp.concatenate([_segsum(sq[:, n * LANES:(n + 1) * LANES], jj_ref)
                          for n in range(RW_WIDTH // LANES)], axis=1)
    kk = kk0 / jnp.maximum(jnp.sqrt(ss), 1e-12)

    r_out[...] = xr
    w_out[...] = w
    k_out[...] = xk * (1.0 + (a - 1.0) * kaw_ref[...])
    v_out[...] = xv
    kk_out[...] = kk
    b_out[...] = kk * a
    g_out[...] = g


def _rwkv_prep(proj, prev_pad, mu_pad, w0, a0, k_k, k_a, w2p, a2p, g2b, n_batch, tm):
    m = proj.shape[0]
    t = m // n_batch
    assert t % tm == 0
    nt = t // tm
    base = 3 * DA_WIDTH // RW_WIDTH
    lbase = (3 * DA_WIDTH + 3 * RW_WIDTH) // LORA_BLOCK
    prev3 = prev_pad.reshape(n_batch, 1, FEAT_PAD)
    mu2 = mu_pad.reshape(1, FEAT_PAD)

    def feat(sec):
        return pl.BlockSpec((tm, RW_WIDTH), lambda b, i, sec=sec: (b * nt + i, base + sec))

    def prev(sec):
        return pl.BlockSpec((None, 1, RW_WIDTH), lambda b, i, sec=sec: (b, 0, sec))

    def mu(sec):
        return pl.BlockSpec((1, RW_WIDTH), lambda b, i, sec=sec: (0, sec))

    vec = pl.BlockSpec((1, RW_WIDTH), lambda b, i: (0, 0))
    out = pl.BlockSpec((tm, RW_WIDTH), lambda b, i: (b * nt + i, 0))
    in_specs = [
        feat(0), feat(1), feat(2),
        pl.BlockSpec((tm, LORA_BLOCK), lambda b, i: (b * nt + i, lbase)),
        prev(0), prev(1), prev(2),
        pl.BlockSpec((None, 1, LORA_BLOCK), lambda b, i: (b, 0, 3 * RW_WIDTH // LORA_BLOCK)),
        mu(0), mu(1), mu(2),
        pl.BlockSpec((1, LORA_BLOCK), lambda b, i: (0, 3 * RW_WIDTH // LORA_BLOCK)),
        vec, vec, vec, vec,
        pl.BlockSpec((LORA_PAD, RW_WIDTH), lambda b, i: (0, 0)),
        pl.BlockSpec((LORA_PAD, RW_WIDTH), lambda b, i: (0, 0)),
        pl.BlockSpec((GATE_LORA, RW_WIDTH), lambda b, i: (0, 0)),
        pl.BlockSpec((2 * LANES, LANES), lambda b, i: (0, 0)),
    ]
    shp = jax.ShapeDtypeStruct((m, RW_WIDTH), F32)
    return pl.pallas_call(
        _rwkv_prep_kernel,
        grid=(n_batch, nt),
        in_specs=in_specs,
        out_specs=[out] * 7,
        out_shape=[shp] * 7,
        scratch_shapes=[pltpu.VMEM((1, RW_WIDTH), F32)] * 3 + [pltpu.VMEM((1, LORA_BLOCK), F32)],
        compiler_params=_cparams("parallel", "arbitrary"),
        name="rwkv_prep",
    )(proj, proj, proj, proj, prev3, prev3, prev3, prev3, mu2, mu2, mu2, mu2,
      w0.reshape(1, -1), a0.reshape(1, -1), k_k.reshape(1, -1), k_a.reshape(1, -1),
      w2p, a2p, g2b, _seg_ones(2))


N_TILES = RW_WIDTH // LANES
S_ROWS = N_TILES * RW_HEAD


def _wkv_kernel(r_ref, w_ref, k_ref, v_ref, kk_ref, b_ref, g_ref, s0_ref,
                lng_ref, lnb_ref, rk_ref, jj2_ref, jj3_ref, dm_ref,
                o_ref, st_ref, s_scr, o_scr):
    tb = pl.program_id(1)
    n_tb = pl.num_programs(1)
    t_blk = r_ref.shape[0]

    @pl.when(tb == 0)
    def _load_state():
        s_scr[...] = s0_ref[...]

    dm = dm_ref[...]
    dm_all = jnp.concatenate([dm] * N_TILES, axis=0)
    dm_all_b = dm_all.astype(BF16)

    def expand(row):
        return jnp.concatenate(
            [jnp.broadcast_to(row[:, n * LANES:(n + 1) * LANES], (RW_HEAD, LANES)) for n in range(N_TILES)],
            axis=0)

    def step(t, carry):
        s = s_scr[...]
        kkx = expand(kk_ref[pl.ds(t, 1), :])
        wx = expand(w_ref[pl.ds(t, 1), :])
        bx = expand(b_ref[pl.ds(t, 1), :])
        kx = expand(k_ref[pl.ds(t, 1), :])
        rx = expand(r_ref[pl.ds(t, 1), :])
        sa = -_segsum(s * kkx, jj2_ref)
        v_row = v_ref[pl.ds(t, 1), :]
        v_hi = v_row.astype(BF16)
        rem = v_row - v_hi.astype(F32)
        v_mid = rem.astype(BF16)
        v_lo = (rem - v_mid.astype(F32)).astype(BF16)
        v_parts = jnp.concatenate([expand(p) * dm_all_b for p in (v_hi, v_mid, v_lo)], axis=1)
        vx = jnp.dot(v_parts, jj3_ref[...], preferred_element_type=F32)
        s_new = s * wx + sa * bx + vx * kx
        s_scr[...] = s_new
        ox = _segsum(s_new * rx, jj2_ref) * dm_all
        o_row = jnp.concatenate(
            [jnp.sum(ox[n * RW_HEAD:(n + 1) * RW_HEAD], axis=0, keepdims=True) for n in range(N_TILES)], axis=1)
        o_scr[pl.ds(t, 1), :] = o_row
        return carry

    lax.fori_loop(0, t_blk, step, 0)

    for n in range(N_TILES):
        sl = slice(n * LANES, (n + 1) * LANES)
        o = o_scr[:, sl]
        mu = _segsum(o, jj2_ref) * (1.0 / RW_HEAD)
        d = o - mu
        var = _segsum(d * d, jj2_ref) * (1.0 / RW_HEAD)
        on = d * lax.rsqrt(var + RW_GN_EPS) * lng_ref[:, sl] + lnb_ref[:, sl]
        bonus = _segsum(r_ref[:, sl] * k_ref[:, sl] * rk_ref[:, sl], jj2_ref) * v_ref[:, sl]
        o_ref[:, sl] = ((on + bonus) * g_ref[:, sl]).astype(o_ref.dtype)

    @pl.when(tb == n_tb - 1)
    def _store_state():
        st_ref[...] = s_scr[...]


def _state_to_tiles(s):
    b = s.shape[0]
    return s.reshape(b, N_TILES, 2, RW_HEAD, RW_HEAD).transpose(0, 1, 3, 2, 4).reshape(b, S_ROWS, LANES)


def _tiles_to_state(s):
    b = s.shape[0]
    return s.reshape(b, N_TILES, RW_HEAD, 2, RW_HEAD).transpose(0, 1, 3, 2, 4).reshape(
        b, RW_HEADS, RW_HEAD, RW_HEAD)


def _wkv(feats, s0_tiles, lnx_g, lnx_b, r_k, n_batch, t_blk):
    r, w, k, v, kk, bb, g = feats
    m = r.shape[0]
    t = m // n_batch
    assert t % t_blk == 0
    nt = t // t_blk
    blk = pl.BlockSpec((t_blk, RW_WIDTH), lambda b, i: (b * nt + i, 0))
    vec = pl.BlockSpec((1, RW_WIDTH), lambda b, i: (0, 0))
    st = pl.BlockSpec((None, S_ROWS, LANES), lambda b, i: (b, 0, 0))
    return pl.pallas_call(
        _wkv_kernel,
        grid=(n_batch, nt),
        in_specs=[blk] * 7 + [st, vec, vec, vec,
                              pl.BlockSpec((2 * LANES, LANES), lambda b, i: (0, 0)),
                              pl.BlockSpec((3 * LANES, LANES), lambda b, i: (0, 0)),
                              pl.BlockSpec((RW_HEAD, LANES), lambda b, i: (0, 0))],
        out_specs=[blk, st],
        out_shape=[jax.ShapeDtypeStruct((m, RW_WIDTH), BF16),
                   jax.ShapeDtypeStruct((n_batch, S_ROWS, LANES), F32)],
        scratch_shapes=[pltpu.VMEM((S_ROWS, LANES), F32), pltpu.VMEM((t_blk, RW_WIDTH), F32)],
        compiler_params=_cparams("parallel", "arbitrary"),
        name="wkv",
    )(r, w, k, v, kk, bb, g, s0_tiles, lnx_g.reshape(1, -1), lnx_b.reshape(1, -1), r_k.reshape(1, -1),
      _seg_ones(2), _seg_ones(3), _diag_mask())


def _outproj_ln_kernel(a1_ref, a2_ref, w_ref, res_ref, g_ref, b_ref, o_ref, acc_scr):
    j = pl.program_id(1)
    nn = pl.num_programs(1)
    k1 = a1_ref.shape[1]
    mix = (jnp.dot(a1_ref[...], w_ref[:k1, :], preferred_element_type=F32)
           + jnp.dot(a2_ref[...], w_ref[k1:, :], preferred_element_type=F32))
    acc_scr[j] = ALPHA * res_ref[...] + mix

    @pl.when(j == nn - 1)
    def _finish():
        y = jnp.concatenate([acc_scr[n] for n in range(acc_scr.shape[0])], axis=1)
        o_ref[...] = _layernorm(y, g_ref[...], b_ref[...])


def _outproj_ln(a1, a2, w, res, g, b, tm, tn):
    m, k1 = a1.shape
    k2 = a2.shape[1]
    n = w.shape[1]
    assert m % tm == 0 and n % tn == 0
    return pl.pallas_call(
        _outproj_ln_kernel,
        grid=(m // tm, n // tn),
        in_specs=[pl.BlockSpec((tm, k1), lambda i, j: (i, 0)),
                  pl.BlockSpec((tm, k2), lambda i, j: (i, 0)),
                  pl.BlockSpec((k1 + k2, tn), lambda i, j: (0, j)),
                  pl.BlockSpec((tm, tn), lambda i, j: (i, j)),
                  pl.BlockSpec((1, n), lambda i, j: (0, 0)),
                  pl.BlockSpec((1, n), lambda i, j: (0, 0))],
        out_specs=pl.BlockSpec((tm, n), lambda i, j: (i, 0)),
        out_shape=jax.ShapeDtypeStruct((m, n), F32),
        scratch_shapes=[pltpu.VMEM((n // tn, tm, tn), F32)],
        compiler_params=_cparams("parallel", "arbitrary"),
        name="outproj_ln1",
    )(a1, a2, w, res, g.reshape(1, -1), b.reshape(1, -1))


def _memattn_kernel(x_ref, mk_ref, mv_ref, wq_ref, wo_ref, g_ref, b_ref, o_ref):
    x = x_ref[...]
    q = jnp.dot(x.astype(BF16), wq_ref[...], preferred_element_type=F32) * (MEM_DH ** -0.5)
    mk = mk_ref[...].astype(BF16)
    mv = mv_ref[...].astype(BF16)
    heads = []
    for h in range(MEM_HEADS):
        sl = slice(h * MEM_DH, (h + 1) * MEM_DH)
        s = lax.dot_general(q[:, sl].astype(BF16), mk[:, sl], _NT, preferred_element_type=F32)
        p = jnp.exp(s - jnp.max(s, axis=-1, keepdims=True))
        l = jnp.sum(p, axis=-1, keepdims=True)
        heads.append(jnp.dot(p.astype(BF16), mv[:, sl], preferred_element_type=F32) / l)
    o = jnp.concatenate(heads, axis=1).astype(BF16)
    y = ALPHA * x + jnp.dot(o, wo_ref[...], preferred_element_type=F32)
    o_ref[...] = _layernorm(y, g_ref[...], b_ref[...])


def _memattn_ln(x, mk, mv, wq, wo, g, b, n_batch, tm):
    m, d = x.shape
    t = m // n_batch
    assert t % tm == 0
    nt = t // tm
    return pl.pallas_call(
        _memattn_kernel,
        grid=(n_batch, nt),
        in_specs=[pl.BlockSpec((tm, d), lambda bi, i: (bi * nt + i, 0)),
                  pl.BlockSpec((None, MEM_LEN, MEM_WIDTH), lambda bi, i: (bi, 0, 0)),
                  pl.BlockSpec((None, MEM_LEN, MEM_WIDTH), lambda bi, i: (bi, 0, 0)),
                  pl.BlockSpec((d, MEM_WIDTH), lambda bi, i: (0, 0)),
                  pl.BlockSpec((MEM_WIDTH, d), lambda bi, i: (0, 0)),
                  pl.BlockSpec((1, d), lambda bi, i: (0, 0)),
                  pl.BlockSpec((1, d), lambda bi, i: (0, 0))],
        out_specs=pl.BlockSpec((tm, d), lambda bi, i: (bi * nt + i, 0)),
        out_shape=jax.ShapeDtypeStruct((m, d), F32),
        compiler_params=_cparams("parallel", "arbitrary"),
        name="memattn_ln2",
    )(x, mk, mv, wq, wo, g.reshape(1, -1), b.reshape(1, -1))


def _swiglu_kernel(x_ref, wg_ref, wu_ref, wd_ref, g_ref, b_ref, o_ref, xb_scr):
    f = pl.program_id(1)
    nf = pl.num_programs(1)

    @pl.when(f == 0)
    def _init():
        x = x_ref[...]
        xb_scr[...] = x.astype(BF16)
        o_ref[...] = ALPHA * x

    xb = xb_scr[...]
    gate = jnp.dot(xb, wg_ref[...], preferred_element_type=F32)
    up = jnp.dot(xb, wu_ref[...], preferred_element_type=F32)
    hidden = (gate * _sigmoid(gate) * up).astype(BF16)
    o_ref[...] += jnp.dot(hidden, wd_ref[...], preferred_element_type=F32)

    @pl.when(f == nf - 1)
    def _finish():
        o_ref[...] = _layernorm(o_ref[...], g_ref[...], b_ref[...])


def _swiglu_ln(x, wg, wu, wd, g, b, tm, tf):
    m, d = x.shape
    f = wg.shape[1]
    assert m % tm == 0 and f % tf == 0
    return pl.pallas_call(
        _swiglu_kernel,
        grid=(m // tm, f // tf),
        in_specs=[pl.BlockSpec((tm, d), lambda i, j: (i, 0), pipeline_mode=pl.Buffered(1)),
                  pl.BlockSpec((d, tf), lambda i, j: (0, j)),
                  pl.BlockSpec((d, tf), lambda i, j: (0, j)),
                  pl.BlockSpec((tf, d), lambda i, j: (j, 0)),
                  pl.BlockSpec((1, d), lambda i, j: (0, 0)),
                  pl.BlockSpec((1, d), lambda i, j: (0, 0))],
        out_specs=pl.BlockSpec((tm, d), lambda i, j: (i, 0), pipeline_mode=pl.Buffered(1)),
        out_shape=jax.ShapeDtypeStruct((m, d), F32),
        scratch_shapes=[pltpu.VMEM((tm, d), BF16)],
        compiler_params=_cparams("parallel", "arbitrary"),
        name="swiglu_ln3",
    )(x, wg, wu, wd, g.reshape(1, -1), b.reshape(1, -1))


def _pad_feat(x):
    c = 3 * RW_WIDTH
    z = jnp.zeros(x.shape[:-1] + (LORA_PAD - DECAY_LORA,), x.dtype)
    return jnp.concatenate([x[..., :c + DECAY_LORA], z, x[..., c + DECAY_LORA:c + DECAY_LORA + AAA_LORA], z,
                            x[..., c + DECAY_LORA + AAA_LORA:]], axis=-1)


def _unpad_feat(x):
    c = 3 * RW_WIDTH
    return jnp.concatenate([x[..., :c + DECAY_LORA], x[..., c + LORA_PAD:c + LORA_PAD + AAA_LORA],
                            x[..., c + 2 * LORA_PAD:]], axis=-1)


def _pad_rows(w, rows):
    return jnp.concatenate([w, jnp.zeros((rows - w.shape[0],) + w.shape[1:], w.dtype)], axis=0)


def _layer(x2d, n_batch, attn_fn, wkv0_tiles, shift0_pad, mem_k, mem_v, p, cfg):
    proj = _matmul(x2d.astype(BF16), p["w_in"], cfg["mm_tm"], cfg["mm_tn"])
    o_a = attn_fn(proj)
    feats = _rwkv_prep(proj, shift0_pad, p["mu_shift"], p["w0"], p["a0"], p["k_k"], p["k_a"],
                       p["w2"], p["a2"], p["g2"], n_batch, cfg["prep_tm"])
    o_r, wkv_t = _wkv(feats, wkv0_tiles, p["lnx_g"], p["lnx_b"], p["r_k"], n_batch, cfg["wkv_tb"])
    x1 = _outproj_ln(o_a, o_r, p["w_out"], x2d, p["ln1_g"], p["ln1_b"], cfg["op_tm"], cfg["op_tn"])
    x2 = _memattn_ln(x1, mem_k, mem_v, p["wq_m"], p["wo_m"], p["ln2_g"], p["ln2_b"], n_batch, cfg["ma_tm"])
    y = _swiglu_ln(x2, p["w_gate"], p["w_up"], p["w_down"], p["ln3_g"], p["ln3_b"], cfg["ff_tm"], cfg["ff_tf"])
    return y, proj, wkv_t


_PROMPT_CFG = dict(mm_tm=1024, mm_tn=512, prep_tm=256, wkv_tb=256, op_tm=512, op_tn=512, ma_tm=256,
                   ff_tm=512, ff_tf=256, attn_tq=512)
_SAMPLE_CFG = dict(mm_tm=256, mm_tn=512, prep_tm=8, wkv_tb=8, op_tm=256, op_tn=512, ma_tm=8,
                   ff_tm=256, ff_tf=256, pages_per_step=4)


def kernel(x_prompt, x_sample, cache_k, cache_v, cache_mem_k, cache_mem_v, state_wkv, state_shift, page_table, mem_prompt, w_in, lam_q1, lam_k1, lam_q2, lam_k2, subln_g, mu_shift, w0, w2, a0, a2, g2, k_k, k_a, r_k, lnx_g, lnx_b, w_out, ln1_g, ln1_b, wq_m, wk_m, wv_m, wo_m, ln2_g, ln2_b, w_gate, w_up, w_down, ln3_g, ln3_b):
    bp, tp, d = x_prompt.shape
    bs, ts, _ = x_sample.shape
    yp = x_prompt.reshape(bp * tp, d)
    ys = x_sample.reshape(bs * ts, d)
    outs = [[] for _ in range(10)]
    for l in range(DEPTH):
        lam_init = 0.8 - 0.6 * math.exp(-0.3 * l)
        c = 3 * DA_WIDTH + 3 * RW_WIDTH
        w_in_l = w_in[l]
        zc = jnp.zeros((d, LORA_PAD - DECAY_LORA), w_in_l.dtype)
        w_in_pad = jnp.concatenate(
            [w_in_l[:, :c + DECAY_LORA], zc, w_in_l[:, c + DECAY_LORA:c + DECAY_LORA + AAA_LORA], zc,
             w_in_l[:, c + DECAY_LORA + AAA_LORA:]], axis=1).astype(BF16)
        p = dict(
            w_in=w_in_pad, mu_shift=_pad_feat(mu_shift[l]), w0=w0[l], a0=a0[l], k_k=k_k[l], k_a=k_a[l],
            w2=_pad_rows(w2[l], LORA_PAD).astype(BF16), a2=_pad_rows(a2[l], LORA_PAD).astype(BF16),
            g2=g2[l].astype(BF16), r_k=r_k[l], lnx_g=lnx_g[l], lnx_b=lnx_b[l],
            w_out=w_out[l].astype(BF16), ln1_g=ln1_g[l], ln1_b=ln1_b[l],
            wq_m=wq_m[l].astype(BF16), wo_m=wo_m[l].astype(BF16), ln2_g=ln2_g[l], ln2_b=ln2_b[l],
            w_gate=w_gate[l].astype(BF16), w_up=w_up[l].astype(BF16), w_down=w_down[l].astype(BF16),
            ln3_g=ln3_g[l], ln3_b=ln3_b[l])
        lamv = jnp.stack([lam_q1[l], lam_k1[l], lam_q2[l], lam_k2[l]])
        g_sub = subln_g[l].reshape(1, DA_HV)

        w_kv = jnp.concatenate([wk_m[l], wv_m[l]], axis=1).astype(BF16)
        mem_kv = _matmul(mem_prompt.reshape(bp * MEM_LEN, d).astype(BF16), w_kv, MEM_LEN, MEM_WIDTH)
        mk_p = mem_kv[:, :MEM_WIDTH].reshape(bp, MEM_LEN, MEM_WIDTH)
        mv_p = mem_kv[:, MEM_WIDTH:].reshape(bp, MEM_LEN, MEM_WIDTH)
        attn_p = functools.partial(_prompt_attention, lamv=lamv, subln_g=g_sub, lam_init=lam_init,
                                   tq=_PROMPT_CFG["attn_tq"])
        yp, proj_p, wkv_p = _layer(yp, bp, attn_p, jnp.zeros((bp, S_ROWS, LANES), F32),
                                   jnp.zeros((bp, FEAT_PAD), F32), mk_p, mv_p, p, _PROMPT_CFG)

        def attn_s(proj, l=l, lamv=lamv, g_sub=g_sub, lam_init=lam_init):
            o = _sample_attention(proj, cache_k[l], cache_v[l], page_table, lamv, g_sub, lam_init, ts,
                                  _SAMPLE_CFG["pages_per_step"])
            return o.reshape(bs * ts, DA_WIDTH).astype(BF16)

        ys, proj_s, wkv_s = _layer(ys, bs, attn_s, _state_to_tiles(state_wkv[l]), _pad_feat(state_shift[l]),
                                   cache_mem_k[l].reshape(bs, MEM_LEN, MEM_WIDTH),
                                   cache_mem_v[l].reshape(bs, MEM_LEN, MEM_WIDTH), p, _SAMPLE_CFG)

        pp3 = proj_p.reshape(bp, tp, PROJ_PAD)
        ps3 = proj_s.reshape(bs, ts, PROJ_PAD)
        outs[0].append(pp3[:, :, DA_WIDTH:2 * DA_WIDTH].reshape(bp, tp, DA_HEADS, DA_HV))
        outs[1].append(pp3[:, :, 2 * DA_WIDTH:3 * DA_WIDTH].reshape(bp, tp, DA_HEADS, DA_HV))
        outs[2].append(_tiles_to_state(wkv_p))
        outs[3].append(_unpad_feat(pp3[:, tp - 1, 3 * DA_WIDTH:]))
        outs[4].append(mk_p.reshape(bp, MEM_LEN, MEM_HEADS, MEM_DH))
        outs[5].append(mv_p.reshape(bp, MEM_LEN, MEM_HEADS, MEM_DH))
        outs[6].append(ps3[:, :, DA_WIDTH:2 * DA_WIDTH].reshape(bs, ts, DA_HEADS, DA_HV))
        outs[7].append(ps3[:, :, 2 * DA_WIDTH:3 * DA_WIDTH].reshape(bs, ts, DA_HEADS, DA_HV))
        outs[8].append(_tiles_to_state(wkv_s))
        outs[9].append(_unpad_feat(ps3[:, ts - 1, 3 * DA_WIDTH:]))
    return (yp.reshape(bp, tp, d), ys.reshape(bs, ts, d)) + tuple(jnp.stack(o) for o in outs)
```

```python
import functools
import math

import jax
import jax.numpy as jnp
import numpy as np
from jax import lax
from jax.experimental import pallas as pl
from jax.experimental.pallas import tpu as pltpu

F32 = jnp.float32
BF16 = jnp.bfloat16

D_MODEL = 4096
DA_HEADS = 8
DA_DH = 128
DA_HV = 2 * DA_DH
DA_WIDTH = DA_HEADS * DA_HV
RW_HEAD = 64
RW_WIDTH = 2048
RW_HEADS = RW_WIDTH // RW_HEAD
DECAY_LORA = 96
AAA_LORA = 96
GATE_LORA = 256
RW_PROJ = 3 * RW_WIDTH + DECAY_LORA + AAA_LORA + GATE_LORA
LORA_PAD = 128
LORA_BLOCK = 2 * LORA_PAD + GATE_LORA
FEAT_PAD = 3 * RW_WIDTH + LORA_BLOCK
PROJ_PAD = 3 * DA_WIDTH + FEAT_PAD
MEM_LEN = 256
MEM_HEADS = 4
MEM_DH = 128
MEM_WIDTH = MEM_HEADS * MEM_DH
PAGE_SIZE = 128
DEPTH = 1
LN_EPS = 1e-5
SUBLN_EPS = 1e-5
RW_GN_EPS = 64e-5
ALPHA = (2.0 * DEPTH) ** 0.25

LANES = 128
VMEM_LIMIT = 56 * 1024 * 1024
NEG_INF = float("-inf")

_NT = (((1,), (1,)), ((), ()))
LOG2E = math.log2(math.e)
QK_SCALE_LOG2 = DA_DH ** -0.5 * LOG2E


def _cparams(*sem):
    return pltpu.CompilerParams(dimension_semantics=sem, vmem_limit_bytes=VMEM_LIMIT)


def _layernorm(x, g, b):
    mu = jnp.mean(x, axis=-1, keepdims=True)
    d = x - mu
    var = jnp.mean(d * d, axis=-1, keepdims=True)
    return d * lax.rsqrt(var + LN_EPS) * g + b


def _sigmoid(x):
    return 1.0 / (1.0 + jnp.exp(-x))


def _split2(x):
    hi = x.astype(BF16)
    lo = (x - hi.astype(F32)).astype(BF16)
    return hi, lo


def _segsum(x, jj_ref):
    hi, lo = _split2(x)
    return jnp.dot(jnp.concatenate([hi, lo], axis=1), jj_ref[...], preferred_element_type=F32)


def _seg_ones(parts):
    r = np.arange(parts * LANES)[:, None] % LANES
    c = np.arange(LANES)[None, :]
    return jnp.asarray((r // RW_HEAD) == (c // RW_HEAD), dtype=BF16)


def _diag_mask():
    r = np.arange(RW_HEAD)[:, None]
    c = np.arange(LANES)[None, :]
    return jnp.asarray((c % RW_HEAD) == r, dtype=F32)


def _mm_kernel(a_ref, w_ref, o_ref):
    o_ref[...] = jnp.dot(a_ref[...], w_ref[...], preferred_element_type=F32)


def _matmul(a, w, tm, tn):
    m, k = a.shape
    n = w.shape[1]
    assert m % tm == 0 and n % tn == 0
    return pl.pallas_call(
        _mm_kernel,
        grid=(m // tm, n // tn),
        in_specs=[pl.BlockSpec((tm, k), lambda i, j: (i, 0)),
                  pl.BlockSpec((k, tn), lambda i, j: (0, j))],
        out_specs=pl.BlockSpec((tm, tn), lambda i, j: (i, j)),
        out_shape=jax.ShapeDtypeStruct((m, n), F32),
        compiler_params=_cparams("parallel", "arbitrary"),
        name="matmul",
    )(a, w)


def _lambda(lamv_ref, lam_init):
    lv = lamv_ref[...]
    s1 = jnp.sum(lv[0:1] * lv[1:2], axis=-1, keepdims=True)
    s2 = jnp.sum(lv[2:3] * lv[3:4], axis=-1, keepdims=True)
    return jnp.exp(s1) - jnp.exp(s2) + lam_init


def _online_softmax_step(s, v, m_ref, l_ref, acc_ref, idx):
    m_old = m_ref[idx]
    m_new = jnp.maximum(m_old, jnp.max(s, axis=-1, keepdims=True))
    corr = jnp.exp2(m_old - m_new)
    p = jnp.exp2(s - m_new)
    l_ref[idx] = l_ref[idx] * corr + jnp.sum(p, axis=-1, keepdims=True)
    acc_ref[idx] = acc_ref[idx] * corr + jnp.dot(p.astype(BF16), v, preferred_element_type=F32)
    m_ref[idx] = m_new


def _diff_combine(acc_ref, l_ref, i1, i2, lam, g, lam_init):
    o = acc_ref[i1] / l_ref[i1] - lam * (acc_ref[i2] / l_ref[i2])
    o = o * lax.rsqrt(jnp.mean(o * o, axis=-1, keepdims=True) + SUBLN_EPS) * g
    return o * (1.0 - lam_init)


def _pattn_kernel(qi_ref, ki_ref, slopes_ref, lamv_ref, g_ref, q_ref, k_ref, v_ref, o_ref,
                  qs_scr, m_scr, l_scr, acc_scr, *, lam_init, tq):
    h = pl.program_id(0)
    st = pl.program_id(1)
    qb = qi_ref[st]
    kb = ki_ref[st]
    slope = slopes_ref[h]

    @pl.when(kb == 0)
    def _init():
        qs_scr[...] = (q_ref[...] * QK_SCALE_LOG2).astype(BF16)
        m_scr[...] = jnp.full(m_scr.shape, NEG_INF, F32)
        l_scr[...] = jnp.zeros(l_scr.shape, F32)
        acc_scr[...] = jnp.zeros(acc_scr.shape, F32)

    def _step(masked):
        k = k_ref[...].astype(BF16)
        v = v_ref[...].astype(BF16)
        col = lax.broadcasted_iota(jnp.int32, (1, tq), 1)
        bias = slope * ((kb - qb) * tq + col).astype(F32)
        if masked:
            row2 = lax.broadcasted_iota(jnp.int32, (tq, tq), 0)
            col2 = lax.broadcasted_iota(jnp.int32, (tq, tq), 1)
            future = col2 > row2
        for c in range(2):
            s = lax.dot_general(qs_scr[:, c * DA_DH:(c + 1) * DA_DH], k[:, c * DA_DH:(c + 1) * DA_DH],
                                _NT, preferred_element_type=F32) + bias
            if masked:
                s = jnp.where(future, NEG_INF, s)
            _online_softmax_step(s, v, m_scr, l_scr, acc_scr, c)

    @pl.when(kb < qb)
    def _off_diag():
        _step(False)

    @pl.when(kb == qb)
    def _diag():
        _step(True)
        lam = _lambda(lamv_ref, lam_init)
        o_ref[...] = _diff_combine(acc_scr, l_scr, 0, 1, lam, g_ref[...], lam_init).astype(o_ref.dtype)


def _prompt_attention(proj, lamv, subln_g, lam_init, tq):
    t = proj.shape[0]
    assert t % tq == 0
    nq = t // tq
    qi = np.concatenate([np.full(q + 1, q, np.int32) for q in range(nq)])
    ki = np.concatenate([np.arange(q + 1, dtype=np.int32) for q in range(nq)])
    slopes = jnp.asarray(LOG2E * 2.0 ** (-8.0 * np.arange(1, DA_HEADS + 1) / DA_HEADS), F32)
    grid_spec = pltpu.PrefetchScalarGridSpec(
        num_scalar_prefetch=3,
        grid=(DA_HEADS, len(qi)),
        in_specs=[
            pl.BlockSpec((4, DA_DH), lambda h, s, qi, ki, sl: (0, 0)),
            pl.BlockSpec((1, DA_HV), lambda h, s, qi, ki, sl: (0, 0)),
            pl.BlockSpec((tq, DA_HV), lambda h, s, qi, ki, sl: (qi[s], h)),
            pl.BlockSpec((tq, DA_HV), lambda h, s, qi, ki, sl: (ki[s], DA_HEADS + h)),
            pl.BlockSpec((tq, DA_HV), lambda h, s, qi, ki, sl: (ki[s], 2 * DA_HEADS + h)),
        ],
        out_specs=pl.BlockSpec((tq, DA_HV), lambda h, s, qi, ki, sl: (qi[s], h)),
        scratch_shapes=[
            pltpu.VMEM((tq, DA_HV), BF16),
            pltpu.VMEM((2, tq, 1), F32),
            pltpu.VMEM((2, tq, 1), F32),
            pltpu.VMEM((2, tq, DA_HV), F32),
        ],
    )
    return pl.pallas_call(
        functools.partial(_pattn_kernel, lam_init=lam_init, tq=tq),
        grid_spec=grid_spec,
        out_shape=jax.ShapeDtypeStruct((t, DA_WIDTH), BF16),
        compiler_params=_cparams("parallel", "arbitrary"),
        name="prompt_attention",
    )(jnp.asarray(qi), jnp.asarray(ki), slopes, lamv, subln_g, proj, proj, proj)


def _sattn_kernel(pt_ref, lamv_ref, g_ref, q_ref, kn_ref, vn_ref, *rest, pages_per_step, lam_init):
    pp = pages_per_step
    kp_refs = rest[:pp]
    vp_refs = rest[pp:2 * pp]
    o_ref = rest[2 * pp]
    qs_scr, bias_scr, m_scr, l_scr, acc_scr = rest[2 * pp + 1:]
    st = pl.program_id(1)
    n_steps = pl.num_programs(1)
    rows = q_ref.shape[0]
    cols = PAGE_SIZE * DA_HEADS
    past_len = n_steps * pp * PAGE_SIZE

    head_r = lax.broadcasted_iota(jnp.int32, (rows, 1), 0) % DA_HEADS
    slope = LOG2E * jnp.exp2((head_r + 1).astype(F32) * (-8.0 / DA_HEADS))

    @pl.when(st == 0)
    def _init():
        qs_scr[...] = (q_ref[...] * QK_SCALE_LOG2).astype(BF16)
        lane = lax.broadcasted_iota(jnp.int32, (rows, cols), 1)
        row = lax.broadcasted_iota(jnp.int32, (rows, cols), 0)
        same_head = (lane % DA_HEADS) == (row % DA_HEADS)
        bias_scr[...] = jnp.where(same_head, slope * (lane // DA_HEADS).astype(F32), NEG_INF)
        m_scr[...] = jnp.full(m_scr.shape, NEG_INF, F32)
        l_scr[...] = jnp.zeros(l_scr.shape, F32)
        acc_scr[...] = jnp.zeros(acc_scr.shape, F32)

    def _block(k_ref, v_ref, bias):
        k = k_ref[...].astype(BF16)
        v = v_ref[...].astype(BF16)
        ps, corrs = [], []
        for c in range(2):
            s = lax.dot_general(qs_scr[:, c * DA_DH:(c + 1) * DA_DH], k[:, c * DA_DH:(c + 1) * DA_DH], _NT,
                                preferred_element_type=F32) + bias
            m_old = m_scr[c]
            m_new = jnp.maximum(m_old, jnp.max(s, axis=-1, keepdims=True))
            corr = jnp.exp2(m_old - m_new)
            p = jnp.exp2(s - m_new)
            l_scr[c] = l_scr[c] * corr + jnp.sum(p, axis=-1, keepdims=True)
            m_scr[c] = m_new
            ps.append(p.astype(BF16))
            corrs.append(corr)
        pv = jnp.dot(jnp.concatenate(ps, axis=0), v, preferred_element_type=F32)
        for c in range(2):
            acc_scr[c] = acc_scr[c] * corrs[c] + pv[c * rows:(c + 1) * rows]

    for p in range(pp):
        offset = ((st * pp + p) * PAGE_SIZE - past_len).astype(F32)
        _block(kp_refs[p], vp_refs[p], bias_scr[...] + slope * offset)

    @pl.when(st == n_steps - 1)
    def _last():
        lane = lax.broadcasted_iota(jnp.int32, (rows, cols), 1)
        row = lax.broadcasted_iota(jnp.int32, (rows, cols), 0)
        causal = (lane // DA_HEADS) <= (row // DA_HEADS)
        _block(kn_ref, vn_ref, jnp.where(causal, bias_scr[...], NEG_INF))
        lam = _lambda(lamv_ref, lam_init)
        o_ref[...] = _diff_combine(acc_scr, l_scr, 0, 1, lam, g_ref[...], lam_init).astype(o_ref.dtype)


def _sample_attention(proj, cache_k, cache_v, page_table, lamv, subln_g, lam_init, n_tok, pages_per_step):
    b, n_pages = page_table.shape
    pp = pages_per_step
    assert n_pages % pp == 0
    n_phys = cache_k.shape[0]
    rows = n_tok * DA_HEADS
    cols = PAGE_SIZE * DA_HEADS
    ck = cache_k.reshape(n_phys, cols, DA_HV)
    cv = cache_v.reshape(n_phys, cols, DA_HV)
    proj3 = proj.reshape(b, n_tok, PROJ_PAD)
    q = proj3[:, :, :DA_WIDTH].reshape(b, rows, DA_HV)
    pad = ((0, 0), (0, PAGE_SIZE - n_tok), (0, 0))
    k_new = jnp.pad(proj3[:, :, DA_WIDTH:2 * DA_WIDTH], pad).reshape(b, cols, DA_HV)
    v_new = jnp.pad(proj3[:, :, 2 * DA_WIDTH:3 * DA_WIDTH], pad).reshape(b, cols, DA_HV)

    def page_spec(p):
        return pl.BlockSpec((None, cols, DA_HV), lambda bi, s, pt, p=p: (pt[bi * n_pages + s * pp + p], 0, 0))

    grid_spec = pltpu.PrefetchScalarGridSpec(
        num_scalar_prefetch=1,
        grid=(b, n_pages // pp),
        in_specs=[
            pl.BlockSpec((4, DA_DH), lambda bi, s, pt: (0, 0)),
            pl.BlockSpec((1, DA_HV), lambda bi, s, pt: (0, 0)),
            pl.BlockSpec((None, rows, DA_HV), lambda bi, s, pt: (bi, 0, 0)),
            pl.BlockSpec((None, cols, DA_HV), lambda bi, s, pt: (bi, 0, 0)),
            pl.BlockSpec((None, cols, DA_HV), lambda bi, s, pt: (bi, 0, 0)),
        ] + [page_spec(p) for p in range(pp)] + [page_spec(p) for p in range(pp)],
        out_specs=pl.BlockSpec((None, rows, DA_HV), lambda bi, s, pt: (bi, 0, 0)),
        scratch_shapes=[
            pltpu.VMEM((rows, DA_HV), BF16),
            pltpu.VMEM((rows, cols), F32),
            pltpu.VMEM((2, rows, 1), F32),
            pltpu.VMEM((2, rows, 1), F32),
            pltpu.VMEM((2, rows, DA_HV), F32),
        ],
    )
    out = pl.pallas_call(
        functools.partial(_sattn_kernel, pages_per_step=pp, lam_init=lam_init),
        grid_spec=grid_spec,
        out_shape=jax.ShapeDtypeStruct((b, rows, DA_HV), F32),
        compiler_params=_cparams("parallel", "arbitrary"),
        name="sample_attention",
    )(page_table.reshape(-1), lamv, subln_g, q, k_new, v_new, *([ck] * pp), *([cv] * pp))
    return out.reshape(b * n_tok, DA_WIDTH)


def _rwkv_prep_kernel(fr_ref, fk_ref, fv_ref, fl_ref, pr_ref, pk_ref, pv_ref, pl_ref,
                      mur_ref, muk_ref, muv_ref, mul_ref, w0_ref, a0_ref, kkw_ref, kaw_ref,
                      w2_ref, a2_ref, g2_ref, jj_ref,
                      r_out, w_out, k_out, v_out, kk_out, b_out, g_out,
                      cr_scr, ck_scr, cv_scr, cl_scr):
    tb = pl.program_id(1)
    tm = fr_ref.shape[0]

    @pl.when(tb == 0)
    def _load_state():
        cr_scr[...] = pr_ref[...]
        ck_scr[...] = pk_ref[...]
        cv_scr[...] = pv_ref[...]
        cl_scr[...] = pl_ref[...]

    def mixed(f_ref, carry_scr, mu_ref):
        x = f_ref[...]
        row = lax.broadcasted_iota(jnp.int32, x.shape, 0)
        shifted = jnp.where(row == 0, carry_scr[...], pltpu.roll(x, 1, 0))
        carry_scr[...] = x[tm - 1:tm, :]
        return x + (shifted - x) * mu_ref[...]

    xr = mixed(fr_ref, cr_scr, mur_ref)
    xk = mixed(fk_ref, ck_scr, muk_ref)
    xv = mixed(fv_ref, cv_scr, muv_ref)
    xl = mixed(fl_ref, cl_scr, mul_ref)

    dw = jnp.tanh(xl[:, 0:LORA_PAD]).astype(BF16)
    da = xl[:, LORA_PAD:2 * LORA_PAD].astype(BF16)
    dg = _sigmoid(xl[:, 2 * LORA_PAD:]).astype(BF16)
    wl = w0_ref[...] + jnp.dot(dw, w2_ref[...], preferred_element_type=F32)
    neg = -wl
    softplus = jnp.maximum(neg, 0.0) + jnp.log(1.0 + jnp.exp(-jnp.abs(neg)))
    w = jnp.exp(-jnp.exp(-softplus - 0.5))
    a = _sigmoid(a0_ref[...] + jnp.dot(da, a2_ref[...], preferred_element_type=F32))
    g = jnp.dot(dg, g2_ref[...], preferred_element_type=F32)

    kk0 = xk * kkw_ref[...]
    sq = kk0 * kk0
    ss = jnp.concatenate([_segsum(sq[:, n * LANES:(n + 1) * LANES], jj_ref)
                          for n in range(RW_WIDTH // LANES)], axis=1)
    kk = kk0 / jnp.maximum(jnp.sqrt(ss), 1e-12)

    r_out[...] = xr
    w_out[...] = w
    k_out[...] = xk * (1.0 + (a - 1.0) * kaw_ref[...])
    v_out[...] = xv
    kk_out[...] = kk
    b_out[...] = kk * a
    g_out[...] = g


def _rwkv_prep(proj, prev_pad, mu_pad, w0, a0, k_k, k_a, w2p, a2p, g2b, n_batch, tm):
    m = proj.shape[0]
    t = m // n_batch
    assert t % tm == 0
    nt = t // tm
    base = 3 * DA_WIDTH // RW_WIDTH
    lbase = (3 * DA_WIDTH + 3 * RW_WIDTH) // LORA_BLOCK
    prev3 = prev_pad.reshape(n_batch, 1, FEAT_PAD)
    mu2 = mu_pad.reshape(1, FEAT_PAD)

    def feat(sec):
        return pl.BlockSpec((tm, RW_WIDTH), lambda b, i, sec=sec: (b * nt + i, base + sec))

    def prev(sec):
        return pl.BlockSpec((None, 1, RW_WIDTH), lambda b, i, sec=sec: (b, 0, sec))

    def mu(sec):
        return pl.BlockSpec((1, RW_WIDTH), lambda b, i, sec=sec: (0, sec))

    vec = pl.BlockSpec((1, RW_WIDTH), lambda b, i: (0, 0))
    out = pl.BlockSpec((tm, RW_WIDTH), lambda b, i: (b * nt + i, 0))
    in_specs = [
        feat(0), feat(1), feat(2),
        pl.BlockSpec((tm, LORA_BLOCK), lambda b, i: (b * nt + i, lbase)),
        prev(0), prev(1), prev(2),
        pl.BlockSpec((None, 1, LORA_BLOCK), lambda b, i: (b, 0, 3 * RW_WIDTH // LORA_BLOCK)),
        mu(0), mu(1), mu(2),
        pl.BlockSpec((1, LORA_BLOCK), lambda b, i: (0, 3 * RW_WIDTH // LORA_BLOCK)),
        vec, vec, vec, vec,
        pl.BlockSpec((LORA_PAD, RW_WIDTH), lambda b, i: (0, 0)),
        pl.BlockSpec((LORA_PAD, RW_WIDTH), lambda b, i: (0, 0)),
        pl.BlockSpec((GATE_LORA, RW_WIDTH), lambda b, i: (0, 0)),
        pl.BlockSpec((2 * LANES, LANES), lambda b, i: (0, 0)),
    ]
    shp = jax.ShapeDtypeStruct((m, RW_WIDTH), F32)
    return pl.pallas_call(
        _rwkv_prep_kernel,
        grid=(n_batch, nt),
        in_specs=in_specs,
        out_specs=[out] * 7,
        out_shape=[shp] * 7,
        scratch_shapes=[pltpu.VMEM((1, RW_WIDTH), F32)] * 3 + [pltpu.VMEM((1, LORA_BLOCK), F32)],
        compiler_params=_cparams("parallel", "arbitrary"),
        name="rwkv_prep",
    )(proj, proj, proj, proj, prev3, prev3, prev3, prev3, mu2, mu2, mu2, mu2,
      w0.reshape(1, -1), a0.reshape(1, -1), k_k.reshape(1, -1), k_a.reshape(1, -1),
      w2p, a2p, g2b, _seg_ones(2))


N_TILES = RW_WIDTH // LANES
S_ROWS = N_TILES * RW_HEAD


def _wkv_kernel(r_ref, w_ref, k_ref, v_ref, kk_ref, b_ref, g_ref, s0_ref,
                lng_ref, lnb_ref, rk_ref, jj2_ref, jj3_ref, dm_ref,
                o_ref, st_ref, s_scr, o_scr):
    tb = pl.program_id(1)
    n_tb = pl.num_programs(1)
    t_blk = r_ref.shape[0]

    @pl.when(tb == 0)
    def _load_state():
        s_scr[...] = s0_ref[...]

    dm = dm_ref[...]
    dm_all = jnp.concatenate([dm] * N_TILES, axis=0)
    dm_all_b = dm_all.astype(BF16)

    def expand(row):
        return jnp.concatenate(
            [jnp.broadcast_to(row[:, n * LANES:(n + 1) * LANES], (RW_HEAD, LANES)) for n in range(N_TILES)],
            axis=0)

    def step(t, carry):
        s = s_scr[...]
        kkx = expand(kk_ref[pl.ds(t, 1), :])
        wx = expand(w_ref[pl.ds(t, 1), :])
        bx = expand(b_ref[pl.ds(t, 1), :])
        kx = expand(k_ref[pl.ds(t, 1), :])
        rx = expand(r_ref[pl.ds(t, 1), :])
        sa = -_segsum(s * kkx, jj2_ref)
        v_row = v_ref[pl.ds(t, 1), :]
        v_hi = v_row.astype(BF16)
        rem = v_row - v_hi.astype(F32)
        v_mid = rem.astype(BF16)
        v_lo = (rem - v_mid.astype(F32)).astype(BF16)
        v_parts = jnp.concatenate([expand(p) * dm_all_b for p in (v_hi, v_mid, v_lo)], axis=1)
        vx = jnp.dot(v_parts, jj3_ref[...], preferred_element_type=F32)
        s_new = s * wx + sa * bx + vx * kx
        s_scr[...] = s_new
        ox = _segsum(s_new * rx, jj2_ref) * dm_all
        o_row = jnp.concatenate(
            [jnp.sum(ox[n * RW_HEAD:(n + 1) * RW_HEAD], axis=0, keepdims=True) for n in range(N_TILES)], axis=1)
        o_scr[pl.ds(t, 1), :] = o_row
        return carry

    lax.fori_loop(0, t_blk, step, 0)

    for n in range(N_TILES):
        sl = slice(n * LANES, (n + 1) * LANES)
        o = o_scr[:, sl]
        mu = _segsum(o, jj2_ref) * (1.0 / RW_HEAD)
        d = o - mu
        var = _segsum(d * d, jj2_ref) * (1.0 / RW_HEAD)
        on = d * lax.rsqrt(var + RW_GN_EPS) * lng_ref[:, sl] + lnb_ref[:, sl]
        bonus = _segsum(r_ref[:, sl] * k_ref[:, sl] * rk_ref[:, sl], jj2_ref) * v_ref[:, sl]
        o_ref[:, sl] = ((on + bonus) * g_ref[:, sl]).astype(o_ref.dtype)

    @pl.when(tb == n_tb - 1)
    def _store_state():
        st_ref[...] = s_scr[...]


def _state_to_tiles(s):
    b = s.shape[0]
    return s.reshape(b, N_TILES, 2, RW_HEAD, RW_HEAD).transpose(0, 1, 3, 2, 4).reshape(b, S_ROWS, LANES)


def _tiles_to_state(s):
    b = s.shape[0]
    return s.reshape(b, N_TILES, RW_HEAD, 2, RW_HEAD).transpose(0, 1, 3, 2, 4).reshape(
        b, RW_HEADS, RW_HEAD, RW_HEAD)


def _wkv(feats, s0_tiles, lnx_g, lnx_b, r_k, n_batch, t_blk):
    r, w, k, v, kk, bb, g = feats
    m = r.shape[0]
    t = m // n_batch
    assert t % t_blk == 0
    nt = t // t_blk
    blk = pl.BlockSpec((t_blk, RW_WIDTH), lambda b, i: (b * nt + i, 0))
    vec = pl.BlockSpec((1, RW_WIDTH), lambda b, i: (0, 0))
    st = pl.BlockSpec((None, S_ROWS, LANES), lambda b, i: (b, 0, 0))
    return pl.pallas_call(
        _wkv_kernel,
        grid=(n_batch, nt),
        in_specs=[blk] * 7 + [st, vec, vec, vec,
                              pl.BlockSpec((2 * LANES, LANES), lambda b, i: (0, 0)),
                              pl.BlockSpec((3 * LANES, LANES), lambda b, i: (0, 0)),
                              pl.BlockSpec((RW_HEAD, LANES), lambda b, i: (0, 0))],
        out_specs=[blk, st],
        out_shape=[jax.ShapeDtypeStruct((m, RW_WIDTH), BF16),
                   jax.ShapeDtypeStruct((n_batch, S_ROWS, LANES), F32)],
        scratch_shapes=[pltpu.VMEM((S_ROWS, LANES), F32), pltpu.VMEM((t_blk, RW_WIDTH), F32)],
        compiler_params=_cparams("parallel", "arbitrary"),
        name="wkv",
    )(r, w, k, v, kk, bb, g, s0_tiles, lnx_g.reshape(1, -1), lnx_b.reshape(1, -1), r_k.reshape(1, -1),
      _seg_ones(2), _seg_ones(3), _diag_mask())


def _outproj_ln_kernel(a1_ref, a2_ref, w_ref, res_ref, g_ref, b_ref, o_ref, acc_scr):
    j = pl.program_id(1)
    nn = pl.num_programs(1)
    k1 = a1_ref.shape[1]
    mix = (jnp.dot(a1_ref[...], w_ref[:k1, :], preferred_element_type=F32)
           + jnp.dot(a2_ref[...], w_ref[k1:, :], preferred_element_type=F32))
    acc_scr[j] = ALPHA * res_ref[...] + mix

    @pl.when(j == nn - 1)
    def _finish():
        y = jnp.concatenate([acc_scr[n] for n in range(acc_scr.shape[0])], axis=1)
        o_ref[...] = _layernorm(y, g_ref[...], b_ref[...])


def _outproj_ln(a1, a2, w, res, g, b, tm, tn):
    m, k1 = a1.shape
    k2 = a2.shape[1]
    n = w.shape[1]
    assert m % tm == 0 and n % tn == 0
    return pl.pallas_call(
        _outproj_ln_kernel,
        grid=(m // tm, n // tn),
        in_specs=[pl.BlockSpec((tm, k1), lambda i, j: (i, 0)),
                  pl.BlockSpec((tm, k2), lambda i, j: (i, 0)),
                  pl.BlockSpec((k1 + k2, tn), lambda i, j: (0, j)),
                  pl.BlockSpec((tm, tn), lambda i, j: (i, j)),
                  pl.BlockSpec((1, n), lambda i, j: (0, 0)),
                  pl.BlockSpec((1, n), lambda i, j: (0, 0))],
        out_specs=pl.BlockSpec((tm, n), lambda i, j: (i, 0)),
        out_shape=jax.ShapeDtypeStruct((m, n), F32),
        scratch_shapes=[pltpu.VMEM((n // tn, tm, tn), F32)],
        compiler_params=_cparams("parallel", "arbitrary"),
        name="outproj_ln1",
    )(a1, a2, w, res, g.reshape(1, -1), b.reshape(1, -1))


def _memattn_kernel(x_ref, mk_ref, mv_ref, wq_ref, wo_ref, g_ref, b_ref, o_ref):
    x = x_ref[...]
    q = jnp.dot(x.astype(BF16), wq_ref[...], preferred_element_type=F32) * (MEM_DH ** -0.5)
    mk = mk_ref[...].astype(BF16)
    mv = mv_ref[...].astype(BF16)
    heads = []
    for h in range(MEM_HEADS):
        sl = slice(h * MEM_DH, (h + 1) * MEM_DH)
        s = lax.dot_general(q[:, sl].astype(BF16), mk[:, sl], _NT, preferred_element_type=F32)
        p = jnp.exp(s - jnp.max(s, axis=-1, keepdims=True))
        l = jnp.sum(p, axis=-1, keepdims=True)
        heads.append(jnp.dot(p.astype(BF16), mv[:, sl], preferred_element_type=F32) / l)
    o = jnp.concatenate(heads, axis=1).astype(BF16)
    y = ALPHA * x + jnp.dot(o, wo_ref[...], preferred_element_type=F32)
    o_ref[...] = _layernorm(y, g_ref[...], b_ref[...])


def _memattn_ln(x, mk, mv, wq, wo, g, b, n_batch, tm):
    m, d = x.shape
    t = m // n_batch
    assert t % tm == 0
    nt = t // tm
    return pl.pallas_call(
        _memattn_kernel,
        grid=(n_batch, nt),
        in_specs=[pl.BlockSpec((tm, d), lambda bi, i: (bi * nt + i, 0)),
                  pl.BlockSpec((None, MEM_LEN, MEM_WIDTH), lambda bi, i: (bi, 0, 0)),
                  pl.BlockSpec((None, MEM_LEN, MEM_WIDTH), lambda bi, i: (bi, 0, 0)),
                  pl.BlockSpec((d, MEM_WIDTH), lambda bi, i: (0, 0)),
                  pl.BlockSpec((MEM_WIDTH, d), lambda bi, i: (0, 0)),
                  pl.BlockSpec((1, d), lambda bi, i: (0, 0)),
                  pl.BlockSpec((1, d), lambda bi, i: (0, 0))],
        out_specs=pl.BlockSpec((tm, d), lambda bi, i: (bi * nt + i, 0)),
        out_shape=jax.ShapeDtypeStruct((m, d), F32),
        compiler_params=_cparams("parallel", "arbitrary"),
        name="memattn_ln2",
    )(x, mk, mv, wq, wo, g.reshape(1, -1), b.reshape(1, -1))


def _swiglu_kernel(x_ref, wg_ref, wu_ref, wd_ref, g_ref, b_ref, o_ref, xb_scr):
    f = pl.program_id(1)
    nf = pl.num_programs(1)

    @pl.when(f == 0)
    def _init():
        x = x_ref[...]
        xb_scr[...] = x.astype(BF16)
        o_ref[...] = ALPHA * x

    xb = xb_scr[...]
    gate = jnp.dot(xb, wg_ref[...], preferred_element_type=F32)
    up = jnp.dot(xb, wu_ref[...], preferred_element_type=F32)
    hidden = (gate * _sigmoid(gate) * up).astype(BF16)
    o_ref[...] += jnp.dot(hidden, wd_ref[...], preferred_element_type=F32)

    @pl.when(f == nf - 1)
    def _finish():
        o_ref[...] = _layernorm(o_ref[...], g_ref[...], b_ref[...])


def _swiglu_ln(x, wg, wu, wd, g, b, tm, tf):
    m, d = x.shape
    f = wg.shape[1]
    assert m % tm == 0 and f % tf == 0
    return pl.pallas_call(
        _swiglu_kernel,
        grid=(m // tm, f // tf),
        in_specs=[pl.BlockSpec((tm, d), lambda i, j: (i, 0), pipeline_mode=pl.Buffered(1)),
                  pl.BlockSpec((d, tf), lambda i, j: (0, j)),
                  pl.BlockSpec((d, tf), lambda i, j: (0, j)),
                  pl.BlockSpec((tf, d), lambda i, j: (j, 0)),
                  pl.BlockSpec((1, d), lambda i, j: (0, 0)),
                  pl.BlockSpec((1, d), lambda i, j: (0, 0))],
        out_specs=pl.BlockSpec((tm, d), lambda i, j: (i, 0), pipeline_mode=pl.Buffered(1)),
        out_shape=jax.ShapeDtypeStruct((m, d), F32),
        scratch_shapes=[pltpu.VMEM((tm, d), BF16)],
        compiler_params=_cparams("parallel", "arbitrary"),
        name="swiglu_ln3",
    )(x, wg, wu, wd, g.reshape(1, -1), b.reshape(1, -1))


def _pad_feat(x):
    c = 3 * RW_WIDTH
    z = jnp.zeros(x.shape[:-1] + (LORA_PAD - DECAY_LORA,), x.dtype)
    return jnp.concatenate([x[..., :c + DECAY_LORA], z, x[..., c + DECAY_LORA:c + DECAY_LORA + AAA_LORA], z,
                            x[..., c + DECAY_LORA + AAA_LORA:]], axis=-1)


def _unpad_feat(x):
    c = 3 * RW_WIDTH
    return jnp.concatenate([x[..., :c + DECAY_LORA], x[..., c + LORA_PAD:c + LORA_PAD + AAA_LORA],
                            x[..., c + 2 * LORA_PAD:]], axis=-1)


def _pad_rows(w, rows):
    return jnp.concatenate([w, jnp.zeros((rows - w.shape[0],) + w.shape[1:], w.dtype)], axis=0)


def _layer(x2d, n_batch, attn_fn, wkv0_tiles, shift0_pad, mem_k, mem_v, p, cfg):
    proj = _matmul(x2d.astype(BF16), p["w_in"], cfg["mm_tm"], cfg["mm_tn"])
    o_a = attn_fn(proj)
    feats = _rwkv_prep(proj, shift0_pad, p["mu_shift"], p["w0"], p["a0"], p["k_k"], p["k_a"],
                       p["w2"], p["a2"], p["g2"], n_batch, cfg["prep_tm"])
    o_r, wkv_t = _wkv(feats, wkv0_tiles, p["lnx_g"], p["lnx_b"], p["r_k"], n_batch, cfg["wkv_tb"])
    x1 = _outproj_ln(o_a, o_r, p["w_out"], x2d, p["ln1_g"], p["ln1_b"], cfg["op_tm"], cfg["op_tn"])
    x2 = _memattn_ln(x1, mem_k, mem_v, p["wq_m"], p["wo_m"], p["ln2_g"], p["ln2_b"], n_batch, cfg["ma_tm"])
    y = _swiglu_ln(x2, p["w_gate"], p["w_up"], p["w_down"], p["ln3_g"], p["ln3_b"], cfg["ff_tm"], cfg["ff_tf"])
    return y, proj, wkv_t


_PROMPT_CFG = dict(mm_tm=1024, mm_tn=512, prep_tm=256, wkv_tb=256, op_tm=512, op_tn=512, ma_tm=256,
                   ff_tm=512, ff_tf=256, attn_tq=512)
_SAMPLE_CFG = dict(mm_tm=256, mm_tn=512, prep_tm=8, wkv_tb=8, op_tm=256, op_tn=512, ma_tm=8,
                   ff_tm=256, ff_tf=256, pages_per_step=8)


def kernel(x_prompt, x_sample, cache_k, cache_v, cache_mem_k, cache_mem_v, state_wkv, state_shift, page_table, mem_prompt, w_in, lam_q1, lam_k1, lam_q2, lam_k2, subln_g, mu_shift, w0, w2, a0, a2, g2, k_k, k_a, r_k, lnx_g, lnx_b, w_out, ln1_g, ln1_b, wq_m, wk_m, wv_m, wo_m, ln2_g, ln2_b, w_gate, w_up, w_down, ln3_g, ln3_b):
    bp, tp, d = x_prompt.shape
    bs, ts, _ = x_sample.shape
    yp = x_prompt.reshape(bp * tp, d)
    ys = x_sample.reshape(bs * ts, d)
    outs = [[] for _ in range(10)]
    for l in range(DEPTH):
        lam_init = 0.8 - 0.6 * math.exp(-0.3 * l)
        c = 3 * DA_WIDTH + 3 * RW_WIDTH
        w_in_l = w_in[l]
        zc = jnp.zeros((d, LORA_PAD - DECAY_LORA), w_in_l.dtype)
        w_in_pad = jnp.concatenate(
            [w_in_l[:, :c + DECAY_LORA], zc, w_in_l[:, c + DECAY_LORA:c + DECAY_LORA + AAA_LORA], zc,
             w_in_l[:, c + DECAY_LORA + AAA_LORA:]], axis=1).astype(BF16)
        p = dict(
            w_in=w_in_pad, mu_shift=_pad_feat(mu_shift[l]), w0=w0[l], a0=a0[l], k_k=k_k[l], k_a=k_a[l],
            w2=_pad_rows(w2[l], LORA_PAD).astype(BF16), a2=_pad_rows(a2[l], LORA_PAD).astype(BF16),
            g2=g2[l].astype(BF16), r_k=r_k[l], lnx_g=lnx_g[l], lnx_b=lnx_b[l],
            w_out=w_out[l].astype(BF16), ln1_g=ln1_g[l], ln1_b=ln1_b[l],
            wq_m=wq_m[l].astype(BF16), wo_m=wo_m[l].astype(BF16), ln2_g=ln2_g[l], ln2_b=ln2_b[l],
            w_gate=w_gate[l].astype(BF16), w_up=w_up[l].astype(BF16), w_down=w_down[l].astype(BF16),
            ln3_g=ln3_g[l], ln3_b=ln3_b[l])
        lamv = jnp.stack([lam_q1[l], lam_k1[l], lam_q2[l], lam_k2[l]])
        g_sub = subln_g[l].reshape(1, DA_HV)

        w_kv = jnp.concatenate([wk_m[l], wv_m[l]], axis=1).astype(BF16)
        mem_kv = _matmul(mem_prompt.reshape(bp * MEM_LEN, d).astype(BF16), w_kv, MEM_LEN, MEM_WIDTH)
        mk_p = mem_kv[:, :MEM_WIDTH].reshape(bp, MEM_LEN, MEM_WIDTH)
        mv_p = mem_kv[:, MEM_WIDTH:].reshape(bp, MEM_LEN, MEM_WIDTH)
        attn_p = functools.partial(_prompt_attention, lamv=lamv, subln_g=g_sub, lam_init=lam_init,
                                   tq=_PROMPT_CFG["attn_tq"])
        yp, proj_p, wkv_p = _layer(yp, bp, attn_p, jnp.zeros((bp, S_ROWS, LANES), F32),
                                   jnp.zeros((bp, FEAT_PAD), F32), mk_p, mv_p, p, _PROMPT_CFG)

        def attn_s(proj, l=l, lamv=lamv, g_sub=g_sub, lam_init=lam_init):
            o = _sample_attention(proj, cache_k[l], cache_v[l], page_table, lamv, g_sub, lam_init, ts,
                                  _SAMPLE_CFG["pages_per_step"])
            return o.astype(BF16)

        ys, proj_s, wkv_s = _layer(ys, bs, attn_s, _state_to_tiles(state_wkv[l]), _pad_feat(state_shift[l]),
                                   cache_mem_k[l].reshape(bs, MEM_LEN, MEM_WIDTH),
                                   cache_mem_v[l].reshape(bs, MEM_LEN, MEM_WIDTH), p, _SAMPLE_CFG)

        pp3 = proj_p.reshape(bp, tp, PROJ_PAD)
        ps3 = proj_s.reshape(bs, ts, PROJ_PAD)
        outs[0].append(pp3[:, :, DA_WIDTH:2 * DA_WIDTH].reshape(bp, tp, DA_HEADS, DA_HV))
        outs[1].append(pp3[:, :, 2 * DA_WIDTH:3 * DA_WIDTH].reshape(bp, tp, DA_HEADS, DA_HV))
        outs[2].append(_tiles_to_state(wkv_p))
        outs[3].append(_unpad_feat(pp3[:, tp - 1, 3 * DA_WIDTH:]))
        outs[4].append(mk_p.reshape(bp, MEM_LEN, MEM_HEADS, MEM_DH))
        outs[5].append(mv_p.reshape(bp, MEM_LEN, MEM_HEADS, MEM_DH))
        outs[6].append(ps3[:, :, DA_WIDTH:2 * DA_WIDTH].reshape(bs, ts, DA_HEADS, DA_HV))
        outs[7].append(ps3[:, :, 2 * DA_WIDTH:3 * DA_WIDTH].reshape(bs, ts, DA_HEADS, DA_HV))
        outs[8].append(_tiles_to_state(wkv_s))
        outs[9].append(_unpad_feat(ps3[:, ts - 1, 3 * DA_WIDTH:]))
    return (yp.reshape(bp, tp, d), ys.reshape(bs, ts, d)) + tuple(jnp.stack(o) for o in outs)
```

```python
import functools
import math

import jax
import jax.numpy as jnp
import numpy as np
from jax import lax
from jax.experimental import pallas as pl
from jax.experimental.pallas import tpu as pltpu

F32 = jnp.float32
BF16 = jnp.bfloat16

D_MODEL = 4096
DA_HEADS = 8
DA_DH = 128
DA_HV = 2 * DA_DH
DA_WIDTH = DA_HEADS * DA_HV
RW_HEAD = 64
RW_WIDTH = 2048
RW_HEADS = RW_WIDTH // RW_HEAD
DECAY_LORA = 96
AAA_LORA = 96
GATE_LORA = 256
RW_PROJ = 3 * RW_WIDTH + DECAY_LORA + AAA_LORA + GATE_LORA
LORA_PAD = 128
LORA_BLOCK = 2 * LORA_PAD + GATE_LORA
FEAT_PAD = 3 * RW_WIDTH + LORA_BLOCK
PROJ_PAD = 3 * DA_WIDTH + FEAT_PAD
MEM_LEN = 256
MEM_HEADS = 4
MEM_DH = 128
MEM_WIDTH = MEM_HEADS * MEM_DH
PAGE_SIZE = 128
DEPTH = 1
LN_EPS = 1e-5
SUBLN_EPS = 1e-5
RW_GN_EPS = 64e-5
ALPHA = (2.0 * DEPTH) ** 0.25

LANES = 128
VMEM_LIMIT = 56 * 1024 * 1024
NEG_INF = float("-inf")

_NT = (((1,), (1,)), ((), ()))
LOG2E = math.log2(math.e)
QK_SCALE_LOG2 = DA_DH ** -0.5 * LOG2E


def _cparams(*sem):
    return pltpu.CompilerParams(dimension_semantics=sem, vmem_limit_bytes=VMEM_LIMIT)


def _layernorm(x, g, b):
    mu = jnp.mean(x, axis=-1, keepdims=True)
    d = x - mu
    var = jnp.mean(d * d, axis=-1, keepdims=True)
    return d * lax.rsqrt(var + LN_EPS) * g + b


def _sigmoid(x):
    return 1.0 / (1.0 + jnp.exp(-x))


def _split2(x):
    hi = x.astype(BF16)
    lo = (x - hi.astype(F32)).astype(BF16)
    return hi, lo


def _segsum(x, jj_ref):
    hi, lo = _split2(x)
    return jnp.dot(jnp.concatenate([hi, lo], axis=1), jj_ref[...], preferred_element_type=F32)


def _seg_ones(parts):
    r = np.arange(parts * LANES)[:, None] % LANES
    c = np.arange(LANES)[None, :]
    return jnp.asarray((r // RW_HEAD) == (c // RW_HEAD), dtype=BF16)


def _diag_mask():
    r = np.arange(RW_HEAD)[:, None]
    c = np.arange(LANES)[None, :]
    return jnp.asarray((c % RW_HEAD) == r, dtype=F32)


def _mm_kernel(a_ref, w_ref, o_ref):
    o_ref[...] = jnp.dot(a_ref[...], w_ref[...], preferred_element_type=F32)


def _matmul(a, w, tm, tn):
    m, k = a.shape
    n = w.shape[1]
    assert m % tm == 0 and n % tn == 0
    return pl.pallas_call(
        _mm_kernel,
        grid=(m // tm, n // tn),
        in_specs=[pl.BlockSpec((tm, k), lambda i, j: (i, 0)),
                  pl.BlockSpec((k, tn), lambda i, j: (0, j))],
        out_specs=pl.BlockSpec((tm, tn), lambda i, j: (i, j)),
        out_shape=jax.ShapeDtypeStruct((m, n), F32),
        compiler_params=_cparams("parallel", "arbitrary"),
        name="matmul",
    )(a, w)


def _lambda(lamv_ref, lam_init):
    lv = lamv_ref[...]
    s1 = jnp.sum(lv[0:1] * lv[1:2], axis=-1, keepdims=True)
    s2 = jnp.sum(lv[2:3] * lv[3:4], axis=-1, keepdims=True)
    return jnp.exp(s1) - jnp.exp(s2) + lam_init


def _online_softmax_step(s, v, m_ref, l_ref, acc_ref, idx):
    m_old = m_ref[idx]
    m_new = jnp.maximum(m_old, jnp.max(s, axis=-1, keepdims=True))
    corr = jnp.exp2(m_old - m_new)
    p = jnp.exp2(s - m_new)
    l_ref[idx] = l_ref[idx] * corr + jnp.sum(p, axis=-1, keepdims=True)
    acc_ref[idx] = acc_ref[idx] * corr + jnp.dot(p.astype(BF16), v, preferred_element_type=F32)
    m_ref[idx] = m_new


def _diff_combine(acc_ref, l_ref, i1, i2, lam, g, lam_init):
    o = acc_ref[i1] / l_ref[i1] - lam * (acc_ref[i2] / l_ref[i2])
    o = o * lax.rsqrt(jnp.mean(o * o, axis=-1, keepdims=True) + SUBLN_EPS) * g
    return o * (1.0 - lam_init)


def _pattn_kernel(qi_ref, ki_ref, slopes_ref, lamv_ref, g_ref, q_ref, k_ref, v_ref, o_ref,
                  qs_scr, m_scr, l_scr, acc_scr, *, lam_init, tq):
    h = pl.program_id(0)
    st = pl.program_id(1)
    qb = qi_ref[st]
    kb = ki_ref[st]
    slope = slopes_ref[h]

    @pl.when(kb == 0)
    def _init():
        qs_scr[...] = (q_ref[...] * QK_SCALE_LOG2).astype(BF16)
        m_scr[...] = jnp.full(m_scr.shape, NEG_INF, F32)
        l_scr[...] = jnp.zeros(l_scr.shape, F32)
        acc_scr[...] = jnp.zeros(acc_scr.shape, F32)

    def _step(masked):
        k = k_ref[...].astype(BF16)
        v = v_ref[...].astype(BF16)
        col = lax.broadcasted_iota(jnp.int32, (1, tq), 1)
        bias = slope * ((kb - qb) * tq + col).astype(F32)
        if masked:
            row2 = lax.broadcasted_iota(jnp.int32, (tq, tq), 0)
            col2 = lax.broadcasted_iota(jnp.int32, (tq, tq), 1)
            future = col2 > row2
        for c in range(2):
            s = lax.dot_general(qs_scr[:, c * DA_DH:(c + 1) * DA_DH], k[:, c * DA_DH:(c + 1) * DA_DH],
                                _NT, preferred_element_type=F32) + bias
            if masked:
                s = jnp.where(future, NEG_INF, s)
            _online_softmax_step(s, v, m_scr, l_scr, acc_scr, c)

    @pl.when(kb < qb)
    def _off_diag():
        _step(False)

    @pl.when(kb == qb)
    def _diag():
        _step(True)
        lam = _lambda(lamv_ref, lam_init)
        o_ref[...] = _diff_combine(acc_scr, l_scr, 0, 1, lam, g_ref[...], lam_init).astype(o_ref.dtype)


def _prompt_attention(proj, lamv, subln_g, lam_init, tq):
    t = proj.shape[0]
    assert t % tq == 0
    nq = t // tq
    qi = np.concatenate([np.full(q + 1, q, np.int32) for q in range(nq)])
    ki = np.concatenate([np.arange(q + 1, dtype=np.int32) for q in range(nq)])
    slopes = jnp.asarray(LOG2E * 2.0 ** (-8.0 * np.arange(1, DA_HEADS + 1) / DA_HEADS), F32)
    grid_spec = pltpu.PrefetchScalarGridSpec(
        num_scalar_prefetch=3,
        grid=(DA_HEADS, len(qi)),
        in_specs=[
            pl.BlockSpec((4, DA_DH), lambda h, s, qi, ki, sl: (0, 0)),
            pl.BlockSpec((1, DA_HV), lambda h, s, qi, ki, sl: (0, 0)),
            pl.BlockSpec((tq, DA_HV), lambda h, s, qi, ki, sl: (qi[s], h)),
            pl.BlockSpec((tq, DA_HV), lambda h, s, qi, ki, sl: (ki[s], DA_HEADS + h)),
            pl.BlockSpec((tq, DA_HV), lambda h, s, qi, ki, sl: (ki[s], 2 * DA_HEADS + h)),
        ],
        out_specs=pl.BlockSpec((tq, DA_HV), lambda h, s, qi, ki, sl: (qi[s], h)),
        scratch_shapes=[
            pltpu.VMEM((tq, DA_HV), BF16),
            pltpu.VMEM((2, tq, 1), F32),
            pltpu.VMEM((2, tq, 1), F32),
            pltpu.VMEM((2, tq, DA_HV), F32),
        ],
    )
    return pl.pallas_call(
        functools.partial(_pattn_kernel, lam_init=lam_init, tq=tq),
        grid_spec=grid_spec,
        out_shape=jax.ShapeDtypeStruct((t, DA_WIDTH), BF16),
        compiler_params=_cparams("parallel", "arbitrary"),
        name="prompt_attention",
    )(jnp.asarray(qi), jnp.asarray(ki), slopes, lamv, subln_g, proj, proj, proj)


def _sattn_kernel(pt_ref, lamv_ref, g_ref, q_ref, kn_ref, vn_ref, *rest, pages_per_step, lam_init):
    pp = pages_per_step
    kp_refs = rest[:pp]
    vp_refs = rest[pp:2 * pp]
    o_ref = rest[2 * pp]
    qs_scr, bias_scr, m_scr, l_scr, acc_scr = rest[2 * pp + 1:]
    st = pl.program_id(1)
    n_steps = pl.num_programs(1)
    rows = q_ref.shape[0]
    cols = PAGE_SIZE * DA_HEADS
    past_len = n_steps * pp * PAGE_SIZE

    head_r = lax.broadcasted_iota(jnp.int32, (rows, 1), 0) % DA_HEADS
    slope = LOG2E * jnp.exp2((head_r + 1).astype(F32) * (-8.0 / DA_HEADS))

    @pl.when(st == 0)
    def _init():
        qs_scr[...] = (q_ref[...] * QK_SCALE_LOG2).astype(BF16)
        lane = lax.broadcasted_iota(jnp.int32, (rows, cols), 1)
        row = lax.broadcasted_iota(jnp.int32, (rows, cols), 0)
        same_head = (lane % DA_HEADS) == (row % DA_HEADS)
        bias_scr[...] = jnp.where(same_head, slope * (lane // DA_HEADS).astype(F32), NEG_INF)
        m_scr[...] = jnp.full(m_scr.shape, NEG_INF, F32)
        l_scr[...] = jnp.zeros(l_scr.shape, F32)
        acc_scr[...] = jnp.zeros(acc_scr.shape, F32)

    def _block(k_ref, v_ref, bias):
        k = k_ref[...].astype(BF16)
        v = v_ref[...].astype(BF16)
        ps, corrs = [], []
        for c in range(2):
            s = lax.dot_general(qs_scr[:, c * DA_DH:(c + 1) * DA_DH], k[:, c * DA_DH:(c + 1) * DA_DH], _NT,
                                preferred_element_type=F32) + bias
            m_old = m_scr[c]
            m_new = jnp.maximum(m_old, jnp.max(s, axis=-1, keepdims=True))
            corr = jnp.exp2(m_old - m_new)
            p = jnp.exp2(s - m_new)
            l_scr[c] = l_scr[c] * corr + jnp.sum(p, axis=-1, keepdims=True)
            m_scr[c] = m_new
            ps.append(p.astype(BF16))
            corrs.append(corr)
        pv = jnp.dot(jnp.concatenate(ps, axis=0), v, preferred_element_type=F32)
        for c in range(2):
            acc_scr[c] = acc_scr[c] * corrs[c] + pv[c * rows:(c + 1) * rows]

    for p in range(pp):
        offset = ((st * pp + p) * PAGE_SIZE - past_len).astype(F32)
        _block(kp_refs[p], vp_refs[p], bias_scr[...] + slope * offset)

    @pl.when(st == n_steps - 1)
    def _last():
        lane = lax.broadcasted_iota(jnp.int32, (rows, cols), 1)
        row = lax.broadcasted_iota(jnp.int32, (rows, cols), 0)
        causal = (lane // DA_HEADS) <= (row // DA_HEADS)
        _block(kn_ref, vn_ref, jnp.where(causal, bias_scr[...], NEG_INF))
        lam = _lambda(lamv_ref, lam_init)
        o_ref[...] = _diff_combine(acc_scr, l_scr, 0, 1, lam, g_ref[...], lam_init).astype(o_ref.dtype)


def _sample_attention(proj, cache_k, cache_v, page_table, lamv, subln_g, lam_init, n_tok, pages_per_step):
    b, n_pages = page_table.shape
    pp = pages_per_step
    assert n_pages % pp == 0
    n_phys = cache_k.shape[0]
    rows = n_tok * DA_HEADS
    cols = PAGE_SIZE * DA_HEADS
    ck = cache_k.reshape(n_phys, cols, DA_HV)
    cv = cache_v.reshape(n_phys, cols, DA_HV)
    proj3 = proj.reshape(b, n_tok, PROJ_PAD)
    q = proj3[:, :, :DA_WIDTH].reshape(b, rows, DA_HV)
    pad = ((0, 0), (0, PAGE_SIZE - n_tok), (0, 0))
    k_new = jnp.pad(proj3[:, :, DA_WIDTH:2 * DA_WIDTH], pad).reshape(b, cols, DA_HV)
    v_new = jnp.pad(proj3[:, :, 2 * DA_WIDTH:3 * DA_WIDTH], pad).reshape(b, cols, DA_HV)

    def page_spec(p):
        return pl.BlockSpec((None, cols, DA_HV), lambda bi, s, pt, p=p: (pt[bi * n_pages + s * pp + p], 0, 0))

    grid_spec = pltpu.PrefetchScalarGridSpec(
        num_scalar_prefetch=1,
        grid=(b, n_pages // pp),
        in_specs=[
            pl.BlockSpec((4, DA_DH), lambda bi, s, pt: (0, 0)),
            pl.BlockSpec((1, DA_HV), lambda bi, s, pt: (0, 0)),
            pl.BlockSpec((None, rows, DA_HV), lambda bi, s, pt: (bi, 0, 0)),
            pl.BlockSpec((None, cols, DA_HV), lambda bi, s, pt: (bi, 0, 0)),
            pl.BlockSpec((None, cols, DA_HV), lambda bi, s, pt: (bi, 0, 0)),
        ] + [page_spec(p) for p in range(pp)] + [page_spec(p) for p in range(pp)],
        out_specs=pl.BlockSpec((None, rows, DA_HV), lambda bi, s, pt: (bi, 0, 0)),
        scratch_shapes=[
            pltpu.VMEM((rows, DA_HV), BF16),
            pltpu.VMEM((rows, cols), F32),
            pltpu.VMEM((2, rows, 1), F32),
            pltpu.VMEM((2, rows, 1), F32),
            pltpu.VMEM((2, rows, DA_HV), F32),
        ],
    )
    out = pl.pallas_call(
        functools.partial(_sattn_kernel, pages_per_step=pp, lam_init=lam_init),
        grid_spec=grid_spec,
        out_shape=jax.ShapeDtypeStruct((b, rows, DA_HV), F32),
        compiler_params=_cparams("parallel", "arbitrary"),
        name="sample_attention",
    )(page_table.reshape(-1), lamv, subln_g, q, k_new, v_new, *([ck] * pp), *([cv] * pp))
    return out.reshape(b * n_tok, DA_WIDTH)


def _rwkv_prep_kernel(fr_ref, fk_ref, fv_ref, fl_ref, pr_ref, pk_ref, pv_ref, pl_ref,
                      mur_ref, muk_ref, muv_ref, mul_ref, w0_ref, a0_ref, kkw_ref, kaw_ref,
                      w2_ref, a2_ref, g2_ref, jj_ref,
                      r_out, w_out, k_out, v_out, kk_out, b_out, g_out,
                      cr_scr, ck_scr, cv_scr, cl_scr):
    tb = pl.program_id(1)
    tm = fr_ref.shape[0]

    @pl.when(tb == 0)
    def _load_state():
        cr_scr[...] = pr_ref[...]
        ck_scr[...] = pk_ref[...]
        cv_scr[...] = pv_ref[...]
        cl_scr[...] = pl_ref[...]

    def mixed(f_ref, carry_scr, mu_ref):
        x = f_ref[...]
        row = lax.broadcasted_iota(jnp.int32, x.shape, 0)
        shifted = jnp.where(row == 0, carry_scr[...], pltpu.roll(x, 1, 0))
        carry_scr[...] = x[tm - 1:tm, :]
        return x + (shifted - x) * mu_ref[...]

    xr = mixed(fr_ref, cr_scr, mur_ref)
    xk = mixed(fk_ref, ck_scr, muk_ref)
    xv = mixed(fv_ref, cv_scr, muv_ref)
    xl = mixed(fl_ref, cl_scr, mul_ref)

    dw = jnp.tanh(xl[:, 0:LORA_PAD]).astype(BF16)
    da = xl[:, LORA_PAD:2 * LORA_PAD].astype(BF16)
    dg = _sigmoid(xl[:, 2 * LORA_PAD:]).astype(BF16)
    wl = w0_ref[...] + jnp.dot(dw, w2_ref[...], preferred_element_type=F32)
    neg = -wl
    softplus = jnp.maximum(neg, 0.0) + jnp.log(1.0 + jnp.exp(-jnp.abs(neg)))
    log_w = -jnp.exp(-softplus - 0.5)
    a = _sigmoid(a0_ref[...] + jnp.dot(da, a2_ref[...], preferred_element_type=F32))
    g = jnp.dot(dg, g2_ref[...], preferred_element_type=F32)

    kk0 = xk * kkw_ref[...]
    sq = kk0 * kk0
    ss = jnp.concatenate([_segsum(sq[:, n * LANES:(n + 1) * LANES], jj_ref)
                          for n in range(RW_WIDTH // LANES)], axis=1)
    kk = kk0 / jnp.maximum(jnp.sqrt(ss), 1e-12)

    r_out[...] = xr
    w_out[...] = log_w
    k_out[...] = xk * (1.0 + (a - 1.0) * kaw_ref[...])
    v_out[...] = xv
    kk_out[...] = kk
    b_out[...] = kk * a
    g_out[...] = g


def _rwkv_prep(proj, prev_pad, mu_pad, w0, a0, k_k, k_a, w2p, a2p, g2b, n_batch, tm):
    m = proj.shape[0]
    t = m // n_batch
    assert t % tm == 0
    nt = t // tm
    base = 3 * DA_WIDTH // RW_WIDTH
    lbase = (3 * DA_WIDTH + 3 * RW_WIDTH) // LORA_BLOCK
    prev3 = prev_pad.reshape(n_batch, 1, FEAT_PAD)
    mu2 = mu_pad.reshape(1, FEAT_PAD)

    def feat(sec):
        return pl.BlockSpec((tm, RW_WIDTH), lambda b, i, sec=sec: (b * nt + i, base + sec))

    def prev(sec):
        return pl.BlockSpec((None, 1, RW_WIDTH), lambda b, i, sec=sec: (b, 0, sec))

    def mu(sec):
        return pl.BlockSpec((1, RW_WIDTH), lambda b, i, sec=sec: (0, sec))

    vec = pl.BlockSpec((1, RW_WIDTH), lambda b, i: (0, 0))
    out = pl.BlockSpec((tm, RW_WIDTH), lambda b, i: (b * nt + i, 0))
    in_specs = [
        feat(0), feat(1), feat(2),
        pl.BlockSpec((tm, LORA_BLOCK), lambda b, i: (b * nt + i, lbase)),
        prev(0), prev(1), prev(2),
        pl.BlockSpec((None, 1, LORA_BLOCK), lambda b, i: (b, 0, 3 * RW_WIDTH // LORA_BLOCK)),
        mu(0), mu(1), mu(2),
        pl.BlockSpec((1, LORA_BLOCK), lambda b, i: (0, 3 * RW_WIDTH // LORA_BLOCK)),
        vec, vec, vec, vec,
        pl.BlockSpec((LORA_PAD, RW_WIDTH), lambda b, i: (0, 0)),
        pl.BlockSpec((LORA_PAD, RW_WIDTH), lambda b, i: (0, 0)),
        pl.BlockSpec((GATE_LORA, RW_WIDTH), lambda b, i: (0, 0)),
        pl.BlockSpec((2 * LANES, LANES), lambda b, i: (0, 0)),
    ]
    shp = jax.ShapeDtypeStruct((m, RW_WIDTH), F32)
    return pl.pallas_call(
        _rwkv_prep_kernel,
        grid=(n_batch, nt),
        in_specs=in_specs,
        out_specs=[out] * 7,
        out_shape=[shp] * 7,
        scratch_shapes=[pltpu.VMEM((1, RW_WIDTH), F32)] * 3 + [pltpu.VMEM((1, LORA_BLOCK), F32)],
        compiler_params=_cparams("parallel", "arbitrary"),
        name="rwkv_prep",
    )(proj, proj, proj, proj, prev3, prev3, prev3, prev3, mu2, mu2, mu2, mu2,
      w0.reshape(1, -1), a0.reshape(1, -1), k_k.reshape(1, -1), k_a.reshape(1, -1),
      w2p, a2p, g2b, _seg_ones(2))


N_TILES = RW_WIDTH // LANES
S_ROWS = N_TILES * RW_HEAD


def _wkv_kernel(r_ref, w_ref, k_ref, v_ref, kk_ref, b_ref, g_ref, s0_ref,
                lng_ref, lnb_ref, rk_ref, jj2_ref, jj3_ref, dm_ref,
                o_ref, st_ref, s_scr, o_scr):
    tb = pl.program_id(1)
    n_tb = pl.num_programs(1)
    t_blk = r_ref.shape[0]

    @pl.when(tb == 0)
    def _load_state():
        s_scr[...] = s0_ref[...]

    dm = dm_ref[...]
    dm_all = jnp.concatenate([dm] * N_TILES, axis=0)
    dm_all_b = dm_all.astype(BF16)

    def expand(row):
        return jnp.concatenate(
            [jnp.broadcast_to(row[:, n * LANES:(n + 1) * LANES], (RW_HEAD, LANES)) for n in range(N_TILES)],
            axis=0)

    def step(t, carry):
        s = s_scr[...]
        kkx = expand(kk_ref[pl.ds(t, 1), :])
        wx = expand(jnp.exp(w_ref[pl.ds(t, 1), :]))
        bx = expand(b_ref[pl.ds(t, 1), :])
        kx = expand(k_ref[pl.ds(t, 1), :])
        rx = expand(r_ref[pl.ds(t, 1), :])
        sa = -_segsum(s * kkx, jj2_ref)
        v_row = v_ref[pl.ds(t, 1), :]
        v_hi = v_row.astype(BF16)
        rem = v_row - v_hi.astype(F32)
        v_mid = rem.astype(BF16)
        v_lo = (rem - v_mid.astype(F32)).astype(BF16)
        v_parts = jnp.concatenate([expand(p) * dm_all_b for p in (v_hi, v_mid, v_lo)], axis=1)
        vx = jnp.dot(v_parts, jj3_ref[...], preferred_element_type=F32)
        s_new = s * wx + sa * bx + vx * kx
        s_scr[...] = s_new
        ox = _segsum(s_new * rx, jj2_ref) * dm_all
        o_row = jnp.concatenate(
            [jnp.sum(ox[n * RW_HEAD:(n + 1) * RW_HEAD], axis=0, keepdims=True) for n in range(N_TILES)], axis=1)
        o_scr[pl.ds(t, 1), :] = o_row
        return carry

    lax.fori_loop(0, t_blk, step, 0)

    for n in range(N_TILES):
        sl = slice(n * LANES, (n + 1) * LANES)
        o = o_scr[:, sl]
        mu = _segsum(o, jj2_ref) * (1.0 / RW_HEAD)
        d = o - mu
        var = _segsum(d * d, jj2_ref) * (1.0 / RW_HEAD)
        on = d * lax.rsqrt(var + RW_GN_EPS) * lng_ref[:, sl] + lnb_ref[:, sl]
        bonus = _segsum(r_ref[:, sl] * k_ref[:, sl] * rk_ref[:, sl], jj2_ref) * v_ref[:, sl]
        o_ref[:, sl] = ((on + bonus) * g_ref[:, sl]).astype(o_ref.dtype)

    @pl.when(tb == n_tb - 1)
    def _store_state():
        st_ref[...] = s_scr[...]


def _state_to_tiles(s):
    b = s.shape[0]
    return s.reshape(b, N_TILES, 2, RW_HEAD, RW_HEAD).transpose(0, 1, 3, 2, 4).reshape(b, S_ROWS, LANES)


def _tiles_to_state(s):
    b = s.shape[0]
    return s.reshape(b, N_TILES, RW_HEAD, 2, RW_HEAD).transpose(0, 1, 3, 2, 4).reshape(
        b, RW_HEADS, RW_HEAD, RW_HEAD)


def _wkv(feats, s0_tiles, lnx_g, lnx_b, r_k, n_batch, t_blk):
    r, w, k, v, kk, bb, g = feats
    m = r.shape[0]
    t = m // n_batch
    assert t % t_blk == 0
    nt = t // t_blk
    blk = pl.BlockSpec((t_blk, RW_WIDTH), lambda b, i: (b * nt + i, 0))
    vec = pl.BlockSpec((1, RW_WIDTH), lambda b, i: (0, 0))
    st = pl.BlockSpec((None, S_ROWS, LANES), lambda b, i: (b, 0, 0))
    return pl.pallas_call(
        _wkv_kernel,
        grid=(n_batch, nt),
        in_specs=[blk] * 7 + [st, vec, vec, vec,
                              pl.BlockSpec((2 * LANES, LANES), lambda b, i: (0, 0)),
                              pl.BlockSpec((3 * LANES, LANES), lambda b, i: (0, 0)),
                              pl.BlockSpec((RW_HEAD, LANES), lambda b, i: (0, 0))],
        out_specs=[blk, st],
        out_shape=[jax.ShapeDtypeStruct((m, RW_WIDTH), BF16),
                   jax.ShapeDtypeStruct((n_batch, S_ROWS, LANES), F32)],
        scratch_shapes=[pltpu.VMEM((S_ROWS, LANES), F32), pltpu.VMEM((t_blk, RW_WIDTH), F32)],
        compiler_params=_cparams("parallel", "arbitrary"),
        name="wkv",
    )(r, w, k, v, kk, bb, g, s0_tiles, lnx_g.reshape(1, -1), lnx_b.reshape(1, -1), r_k.reshape(1, -1),
      _seg_ones(2), _seg_ones(3), _diag_mask())


_NN = (((1,), (0,)), ((), ()))


def _mm3(a, b, dims=_NN):
    f = lambda x, y: lax.dot_general(x, y, dims, preferred_element_type=F32)
    return f(a[0], b[0]) + (f(a[0], b[1]) + f(a[1], b[0]))


def _wkv_chunk_tiles(lw, r, k, v, kk, b, ys_in, c_len):
    rows = 2 * c_len
    lane = lax.broadcasted_iota(jnp.int32, (1, LANES), 1)
    head_a = (lane < RW_HEAD).astype(F32)
    head_b = 1.0 - head_a

    def stack(x):
        return jnp.concatenate([x * head_a, x * head_b], axis=0)

    def each(fn, *lists):
        return [fn(*args) for args in zip(*lists)]

    tr = lax.broadcasted_iota(jnp.int32, (c_len, c_len), 0)
    tc = lax.broadcasted_iota(jnp.int32, (c_len, c_len), 1)
    tril = (tc <= tr).astype(BF16)

    def cumsum(x):
        l1 = x.astype(BF16)
        rem = x - l1.astype(F32)
        l2 = rem.astype(BF16)
        l3 = (rem - l2.astype(F32)).astype(BF16)
        return (jnp.dot(tril, l1, preferred_element_type=F32) + jnp.dot(tril, l2, preferred_element_type=F32)
                + jnp.dot(tril, l3, preferred_element_type=F32))

    cs = each(cumsum, lw)
    cs_last = [c[c_len - 1:c_len, :] for c in cs]
    g_rem = each(lambda c, cl: jnp.exp(cl - c), cs, cs_last)
    g_inv = each(lambda c: jnp.exp(-c), cs)
    a2f = each(lambda x, c, l: stack(-x * jnp.exp(c - l)), kk, cs, lw)
    r2f = each(lambda x, c: stack(x * jnp.exp(c)), r, cs)
    v2f = each(stack, v)
    a2 = each(_split2, a2f)
    r2 = each(_split2, r2f)
    v2 = each(_split2, v2f)
    b2 = each(lambda x, g: _split2(stack(x * g)), b, g_inv)
    k2 = each(lambda x, g: _split2(stack(x * g)), k, g_inv)
    bg2 = each(lambda x, g: _split2(stack(x * g)), b, g_rem)
    kg2 = each(lambda x, g: _split2(stack(x * g)), k, g_rem)

    rr = lax.broadcasted_iota(jnp.int32, (rows, rows), 0)
    cc = lax.broadcasted_iota(jnp.int32, (rows, rows), 1)
    same = (rr // c_len) == (cc // c_len)
    strict = same & ((cc % c_len) < (rr % c_len))
    lower = same & ((cc % c_len) <= (rr % c_len))
    eye = jnp.where(rr == cc, 1.0, 0.0)

    l_mat = each(lambda x, y: jnp.where(strict, _mm3(x, y, _NT), 0.0), a2, b2)
    ak = each(lambda x, y: jnp.where(strict, _mm3(x, y, _NT), 0.0), a2, k2)
    rb = each(lambda x, y: jnp.where(lower, _mm3(x, y, _NT), 0.0), r2, b2)
    rk = each(lambda x, y: jnp.where(lower, _mm3(x, y, _NT), 0.0), r2, k2)

    t_mat = [eye + x for x in l_mat]
    p_mat = l_mat
    for _ in range(max(int(math.log2(c_len)) - 1, 0)):
        ps = each(_split2, p_mat)
        p_mat = each(lambda x: _mm3(x, x), ps)
        t_mat = each(lambda t, p: t + _mm3(_split2(t), _split2(p)), t_mat, p_mat)

    akv = each(lambda x, y: _mm3(_split2(x), y), ak, v2)
    wu = each(lambda t, x, y: _mm3(_split2(t), _split2(jnp.concatenate([x, y], axis=1))),
              t_mat, a2f, akv)
    wus = each(_split2, wu)
    qo = each(lambda x, y: _mm3(_split2(x), y), rb, wus)
    q2 = each(lambda x, y: x + y[:, :LANES], r2f, qo)
    o0 = each(lambda q, x, y: q[:, LANES:] + _mm3(_split2(x), y), qo, rk, v2)
    mn = each(lambda x, y: _mm3(_split2(x.T), y), wu, bg2)
    n_mat = each(lambda m, x, y: m[LANES:] + _mm3(_split2(x.T), y), mn, v2f, kg2)

    ys = each(_split2, ys_in)
    o2 = each(lambda q, y, o: _mm3(_split2(q), y, _NT) + o, q2, ys, o0)
    o = [x[:c_len] + x[c_len:] for x in o2]
    y_new = each(lambda y, cl, ysp, m, nm: y * jnp.exp(cl) + _mm3(ysp, _split2(m[:LANES])) + nm,
                 ys_in, cs_last, ys, mn, n_mat)
    return o, y_new


def _wkv_chunk_kernel(lw_ref, r_ref, k_ref, v_ref, kk_ref, b_ref, g_ref, y0_ref,
                      lng_ref, lnb_ref, rk_ref, jj2_ref, o_ref, yt_ref, y_scr):
    c = pl.program_id(2)
    n_c = pl.num_programs(2)
    c_len = r_ref.shape[0]
    tiles = r_ref.shape[1] // LANES

    @pl.when(c == 0)
    def _load_state():
        y_scr[...] = y0_ref[...]

    sls = [slice(t * LANES, (t + 1) * LANES) for t in range(tiles)]
    split = lambda ref: [ref[:, sl] for sl in sls]
    r, k, v = split(r_ref), split(k_ref), split(v_ref)
    o, y_new = _wkv_chunk_tiles(split(lw_ref), r, k, v, split(kk_ref), split(b_ref),
                                [y_scr[t] for t in range(tiles)], c_len)
    for t, sl in enumerate(sls):
        y_scr[t] = y_new[t]
        mu = _segsum(o[t], jj2_ref) * (1.0 / RW_HEAD)
        d = o[t] - mu
        var = _segsum(d * d, jj2_ref) * (1.0 / RW_HEAD)
        on = d * lax.rsqrt(var + RW_GN_EPS) * lng_ref[:, sl] + lnb_ref[:, sl]
        bonus = _segsum(r[t] * k[t] * rk_ref[:, sl], jj2_ref) * v[t]
        o_ref[:, sl] = ((on + bonus) * g_ref[:, sl]).astype(o_ref.dtype)

    @pl.when(c == n_c - 1)
    def _store_state():
        yt_ref[...] = y_scr[...]


def _state_to_blockdiag(s):
    b = s.shape[0]
    s5 = s.reshape(b, N_TILES, 2, RW_HEAD, RW_HEAD)
    z = jnp.zeros((b, N_TILES, RW_HEAD, RW_HEAD), s.dtype)
    top = jnp.concatenate([s5[:, :, 0], z], axis=-1)
    bot = jnp.concatenate([z, s5[:, :, 1]], axis=-1)
    return jnp.concatenate([top, bot], axis=-2)


def _blockdiag_to_state(y):
    b = y.shape[0]
    s = jnp.stack([y[:, :, :RW_HEAD, :RW_HEAD], y[:, :, RW_HEAD:, RW_HEAD:]], axis=2)
    return s.reshape(b, RW_HEADS, RW_HEAD, RW_HEAD)


def _wkv_chunked(feats, y0, lnx_g, lnx_b, r_k, n_batch, c_len, tiles):
    r, lw, k, v, kk, bb, g = feats
    m = r.shape[0]
    t = m // n_batch
    assert t % c_len == 0 and N_TILES % tiles == 0
    nc = t // c_len
    wid = tiles * LANES
    blk = pl.BlockSpec((c_len, wid), lambda b, gi, c: (b * nc + c, gi))
    vec = pl.BlockSpec((1, wid), lambda b, gi, c: (0, gi))
    st = pl.BlockSpec((None, tiles, LANES, LANES), lambda b, gi, c: (b, gi, 0, 0))
    return pl.pallas_call(
        _wkv_chunk_kernel,
        grid=(n_batch, N_TILES // tiles, nc),
        in_specs=[blk] * 7 + [st, vec, vec, vec, pl.BlockSpec((2 * LANES, LANES), lambda b, gi, c: (0, 0))],
        out_specs=[blk, st],
        out_shape=[jax.ShapeDtypeStruct((m, RW_WIDTH), BF16),
                   jax.ShapeDtypeStruct((n_batch, N_TILES, LANES, LANES), F32)],
        scratch_shapes=[pltpu.VMEM((tiles, LANES, LANES), F32)],
        compiler_params=_cparams("parallel", "parallel", "arbitrary"),
        name="wkv_chunked",
    )(lw, r, k, v, kk, bb, g, y0, lnx_g.reshape(1, -1), lnx_b.reshape(1, -1), r_k.reshape(1, -1), _seg_ones(2))


def _outproj_ln_kernel(a1_ref, a2_ref, w_ref, res_ref, g_ref, b_ref, o_ref, acc_scr):
    j = pl.program_id(1)
    nn = pl.num_programs(1)
    k1 = a1_ref.shape[1]
    mix = (jnp.dot(a1_ref[...], w_ref[:k1, :], preferred_element_type=F32)
           + jnp.dot(a2_ref[...], w_ref[k1:, :], preferred_element_type=F32))
    acc_scr[j] = ALPHA * res_ref[...] + mix

    @pl.when(j == nn - 1)
    def _finish():
        y = jnp.concatenate([acc_scr[n] for n in range(acc_scr.shape[0])], axis=1)
        o_ref[...] = _layernorm(y, g_ref[...], b_ref[...])


def _outproj_ln(a1, a2, w, res, g, b, tm, tn):
    m, k1 = a1.shape
    k2 = a2.shape[1]
    n = w.shape[1]
    assert m % tm == 0 and n % tn == 0
    return pl.pallas_call(
        _outproj_ln_kernel,
        grid=(m // tm, n // tn),
        in_specs=[pl.BlockSpec((tm, k1), lambda i, j: (i, 0)),
                  pl.BlockSpec((tm, k2), lambda i, j: (i, 0)),
                  pl.BlockSpec((k1 + k2, tn), lambda i, j: (0, j)),
                  pl.BlockSpec((tm, tn), lambda i, j: (i, j)),
                  pl.BlockSpec((1, n), lambda i, j: (0, 0)),
                  pl.BlockSpec((1, n), lambda i, j: (0, 0))],
        out_specs=pl.BlockSpec((tm, n), lambda i, j: (i, 0)),
        out_shape=jax.ShapeDtypeStruct((m, n), F32),
        scratch_shapes=[pltpu.VMEM((n // tn, tm, tn), F32)],
        compiler_params=_cparams("parallel", "arbitrary"),
        name="outproj_ln1",
    )(a1, a2, w, res, g.reshape(1, -1), b.reshape(1, -1))


def _memattn_kernel(x_ref, mk_ref, mv_ref, wq_ref, wo_ref, g_ref, b_ref, o_ref):
    x = x_ref[...]
    q = jnp.dot(x.astype(BF16), wq_ref[...], preferred_element_type=F32) * (MEM_DH ** -0.5)
    mk = mk_ref[...].astype(BF16)
    mv = mv_ref[...].astype(BF16)
    heads = []
    for h in range(MEM_HEADS):
        sl = slice(h * MEM_DH, (h + 1) * MEM_DH)
        s = lax.dot_general(q[:, sl].astype(BF16), mk[:, sl], _NT, preferred_element_type=F32)
        p = jnp.exp(s - jnp.max(s, axis=-1, keepdims=True))
        l = jnp.sum(p, axis=-1, keepdims=True)
        heads.append(jnp.dot(p.astype(BF16), mv[:, sl], preferred_element_type=F32) / l)
    o = jnp.concatenate(heads, axis=1).astype(BF16)
    y = ALPHA * x + jnp.dot(o, wo_ref[...], preferred_element_type=F32)
    o_ref[...] = _layernorm(y, g_ref[...], b_ref[...])


def _memattn_ln(x, mk, mv, wq, wo, g, b, n_batch, tm):
    m, d = x.shape
    t = m // n_batch
    assert t % tm == 0
    nt = t // tm
    return pl.pallas_call(
        _memattn_kernel,
        grid=(n_batch, nt),
        in_specs=[pl.BlockSpec((tm, d), lambda bi, i: (bi * nt + i, 0)),
                  pl.BlockSpec((None, MEM_LEN, MEM_WIDTH), lambda bi, i: (bi, 0, 0)),
                  pl.BlockSpec((None, MEM_LEN, MEM_WIDTH), lambda bi, i: (bi, 0, 0)),
                  pl.BlockSpec((d, MEM_WIDTH), lambda bi, i: (0, 0)),
                  pl.BlockSpec((MEM_WIDTH, d), lambda bi, i: (0, 0)),
                  pl.BlockSpec((1, d), lambda bi, i: (0, 0)),
                  pl.BlockSpec((1, d), lambda bi, i: (0, 0))],
        out_specs=pl.BlockSpec((tm, d), lambda bi, i: (bi * nt + i, 0)),
        out_shape=jax.ShapeDtypeStruct((m, d), F32),
        compiler_params=_cparams("parallel", "arbitrary"),
        name="memattn_ln2",
    )(x, mk, mv, wq, wo, g.reshape(1, -1), b.reshape(1, -1))


def _swiglu_kernel(x_ref, wg_ref, wu_ref, wd_ref, g_ref, b_ref, o_ref, xb_scr):
    f = pl.program_id(1)
    nf = pl.num_programs(1)

    @pl.when(f == 0)
    def _init():
        x = x_ref[...]
        xb_scr[...] = x.astype(BF16)
        o_ref[...] = ALPHA * x

    xb = xb_scr[...]
    gate = jnp.dot(xb, wg_ref[...], preferred_element_type=F32)
    up = jnp.dot(xb, wu_ref[...], preferred_element_type=F32)
    hidden = (gate * _sigmoid(gate) * up).astype(BF16)
    o_ref[...] += jnp.dot(hidden, wd_ref[...], preferred_element_type=F32)

    @pl.when(f == nf - 1)
    def _finish():
        o_ref[...] = _layernorm(o_ref[...], g_ref[...], b_ref[...])


def _swiglu_ln(x, wg, wu, wd, g, b, tm, tf):
    m, d = x.shape
    f = wg.shape[1]
    assert m % tm == 0 and f % tf == 0
    return pl.pallas_call(
        _swiglu_kernel,
        grid=(m // tm, f // tf),
        in_specs=[pl.BlockSpec((tm, d), lambda i, j: (i, 0), pipeline_mode=pl.Buffered(1)),
                  pl.BlockSpec((d, tf), lambda i, j: (0, j)),
                  pl.BlockSpec((d, tf), lambda i, j: (0, j)),
                  pl.BlockSpec((tf, d), lambda i, j: (j, 0)),
                  pl.BlockSpec((1, d), lambda i, j: (0, 0)),
                  pl.BlockSpec((1, d), lambda i, j: (0, 0))],
        out_specs=pl.BlockSpec((tm, d), lambda i, j: (i, 0), pipeline_mode=pl.Buffered(1)),
        out_shape=jax.ShapeDtypeStruct((m, d), F32),
        scratch_shapes=[pltpu.VMEM((tm, d), BF16)],
        compiler_params=_cparams("parallel", "arbitrary"),
        name="swiglu_ln3",
    )(x, wg, wu, wd, g.reshape(1, -1), b.reshape(1, -1))


def _pad_feat(x):
    c = 3 * RW_WIDTH
    z = jnp.zeros(x.shape[:-1] + (LORA_PAD - DECAY_LORA,), x.dtype)
    return jnp.concatenate([x[..., :c + DECAY_LORA], z, x[..., c + DECAY_LORA:c + DECAY_LORA + AAA_LORA], z,
                            x[..., c + DECAY_LORA + AAA_LORA:]], axis=-1)


def _unpad_feat(x):
    c = 3 * RW_WIDTH
    return jnp.concatenate([x[..., :c + DECAY_LORA], x[..., c + LORA_PAD:c + LORA_PAD + AAA_LORA],
                            x[..., c + 2 * LORA_PAD:]], axis=-1)


def _pad_rows(w, rows):
    return jnp.concatenate([w, jnp.zeros((rows - w.shape[0],) + w.shape[1:], w.dtype)], axis=0)


def _layer(x2d, n_batch, attn_fn, wkv0, shift0_pad, mem_k, mem_v, p, cfg):
    proj = _matmul(x2d.astype(BF16), p["w_in"], cfg["mm_tm"], cfg["mm_tn"])
    o_a = attn_fn(proj)
    feats = _rwkv_prep(proj, shift0_pad, p["mu_shift"], p["w0"], p["a0"], p["k_k"], p["k_a"],
                       p["w2"], p["a2"], p["g2"], n_batch, cfg["prep_tm"])
    if cfg["wkv_chunk"]:
        o_r, y_t = _wkv_chunked(feats, _state_to_blockdiag(wkv0), p["lnx_g"], p["lnx_b"], p["r_k"], n_batch,
                                cfg["wkv_chunk"], cfg["wkv_tiles"])
        wkv_t = _blockdiag_to_state(y_t)
    else:
        o_r, s_t = _wkv(feats, _state_to_tiles(wkv0), p["lnx_g"], p["lnx_b"], p["r_k"], n_batch, cfg["wkv_tb"])
        wkv_t = _tiles_to_state(s_t)
    x1 = _outproj_ln(o_a, o_r, p["w_out"], x2d, p["ln1_g"], p["ln1_b"], cfg["op_tm"], cfg["op_tn"])
    x2 = _memattn_ln(x1, mem_k, mem_v, p["wq_m"], p["wo_m"], p["ln2_g"], p["ln2_b"], n_batch, cfg["ma_tm"])
    y = _swiglu_ln(x2, p["w_gate"], p["w_up"], p["w_down"], p["ln3_g"], p["ln3_b"], cfg["ff_tm"], cfg["ff_tf"])
    return y, proj, wkv_t


_PROMPT_CFG = dict(mm_tm=1024, mm_tn=512, prep_tm=256, wkv_chunk=64, wkv_tiles=8, op_tm=512, op_tn=512,
                   ma_tm=256, ff_tm=512, ff_tf=256, attn_tq=512)
_SAMPLE_CFG = dict(mm_tm=256, mm_tn=512, prep_tm=8, wkv_chunk=0, wkv_tb=8, op_tm=256, op_tn=512, ma_tm=8,
                   ff_tm=256, ff_tf=256, pages_per_step=8)


def kernel(x_prompt, x_sample, cache_k, cache_v, cache_mem_k, cache_mem_v, state_wkv, state_shift, page_table, mem_prompt, w_in, lam_q1, lam_k1, lam_q2, lam_k2, subln_g, mu_shift, w0, w2, a0, a2, g2, k_k, k_a, r_k, lnx_g, lnx_b, w_out, ln1_g, ln1_b, wq_m, wk_m, wv_m, wo_m, ln2_g, ln2_b, w_gate, w_up, w_down, ln3_g, ln3_b):
    bp, tp, d = x_prompt.shape
    bs, ts, _ = x_sample.shape
    yp = x_prompt.reshape(bp * tp, d)
    ys = x_sample.reshape(bs * ts, d)
    outs = [[] for _ in range(10)]
    for l in range(DEPTH):
        lam_init = 0.8 - 0.6 * math.exp(-0.3 * l)
        c = 3 * DA_WIDTH + 3 * RW_WIDTH
        w_in_l = w_in[l]
        zc = jnp.zeros((d, LORA_PAD - DECAY_LORA), w_in_l.dtype)
        w_in_pad = jnp.concatenate(
            [w_in_l[:, :c + DECAY_LORA], zc, w_in_l[:, c + DECAY_LORA:c + DECAY_LORA + AAA_LORA], zc,
             w_in_l[:, c + DECAY_LORA + AAA_LORA:]], axis=1).astype(BF16)
        p = dict(
            w_in=w_in_pad, mu_shift=_pad_feat(mu_shift[l]), w0=w0[l], a0=a0[l], k_k=k_k[l], k_a=k_a[l],
            w2=_pad_rows(w2[l], LORA_PAD).astype(BF16), a2=_pad_rows(a2[l], LORA_PAD).astype(BF16),
            g2=g2[l].astype(BF16), r_k=r_k[l], lnx_g=lnx_g[l], lnx_b=lnx_b[l],
            w_out=w_out[l].astype(BF16), ln1_g=ln1_g[l], ln1_b=ln1_b[l],
            wq_m=wq_m[l].astype(BF16), wo_m=wo_m[l].astype(BF16), ln2_g=ln2_g[l], ln2_b=ln2_b[l],
            w_gate=w_gate[l].astype(BF16), w_up=w_up[l].astype(BF16), w_down=w_down[l].astype(BF16),
            ln3_g=ln3_g[l], ln3_b=ln3_b[l])
        lamv = jnp.stack([lam_q1[l], lam_k1[l], lam_q2[l], lam_k2[l]])
        g_sub = subln_g[l].reshape(1, DA_HV)

        w_kv = jnp.concatenate([wk_m[l], wv_m[l]], axis=1).astype(BF16)
        mem_kv = _matmul(mem_prompt.reshape(bp * MEM_LEN, d).astype(BF16), w_kv, MEM_LEN, MEM_WIDTH)
        mk_p = mem_kv[:, :MEM_WIDTH].reshape(bp, MEM_LEN, MEM_WIDTH)
        mv_p = mem_kv[:, MEM_WIDTH:].reshape(bp, MEM_LEN, MEM_WIDTH)
        attn_p = functools.partial(_prompt_attention, lamv=lamv, subln_g=g_sub, lam_init=lam_init,
                                   tq=_PROMPT_CFG["attn_tq"])
        yp, proj_p, wkv_p = _layer(yp, bp, attn_p, jnp.zeros((bp, RW_HEADS, RW_HEAD, RW_HEAD), F32),
                                   jnp.zeros((bp, FEAT_PAD), F32), mk_p, mv_p, p, _PROMPT_CFG)

        def attn_s(proj, l=l, lamv=lamv, g_sub=g_sub, lam_init=lam_init):
            o = _sample_attention(proj, cache_k[l], cache_v[l], page_table, lamv, g_sub, lam_init, ts,
                                  _SAMPLE_CFG["pages_per_step"])
            return o.astype(BF16)

        ys, proj_s, wkv_s = _layer(ys, bs, attn_s, state_wkv[l], _pad_feat(state_shift[l]),
                                   cache_mem_k[l].reshape(bs, MEM_LEN, MEM_WIDTH),
                                   cache_mem_v[l].reshape(bs, MEM_LEN, MEM_WIDTH), p, _SAMPLE_CFG)

        pp3 = proj_p.reshape(bp, tp, PROJ_PAD)
        ps3 = proj_s.reshape(bs, ts, PROJ_PAD)
        outs[0].append(pp3[:, :, DA_WIDTH:2 * DA_WIDTH].reshape(bp, tp, DA_HEADS, DA_HV))
        outs[1].append(pp3[:, :, 2 * DA_WIDTH:3 * DA_WIDTH].reshape(bp, tp, DA_HEADS, DA_HV))
        outs[2].append(wkv_p)
        outs[3].append(_unpad_feat(pp3[:, tp - 1, 3 * DA_WIDTH:]))
        outs[4].append(mk_p.reshape(bp, MEM_LEN, MEM_HEADS, MEM_DH))
        outs[5].append(mv_p.reshape(bp, MEM_LEN, MEM_HEADS, MEM_DH))
        outs[6].append(ps3[:, :, DA_WIDTH:2 * DA_WIDTH].reshape(bs, ts, DA_HEADS, DA_HV))
        outs[7].append(ps3[:, :, 2 * DA_WIDTH:3 * DA_WIDTH].reshape(bs, ts, DA_HEADS, DA_HV))
        outs[8].append(wkv_s)
        outs[9].append(_unpad_feat(ps3[:, ts - 1, 3 * DA_WIDTH:]))
    return (yp.reshape(bp, tp, d), ys.reshape(bs, ts, d)) + tuple(jnp.stack(o) for o in outs)
```

```python
import functools
import math

import jax
import jax.numpy as jnp
import numpy as np
from jax import lax
from jax.experimental import pallas as pl
from jax.experimental.pallas import tpu as pltpu

F32 = jnp.float32
BF16 = jnp.bfloat16

D_MODEL = 4096
DA_HEADS = 8
DA_DH = 128
DA_HV = 2 * DA_DH
DA_WIDTH = DA_HEADS * DA_HV
RW_HEAD = 64
RW_WIDTH = 2048
RW_HEADS = RW_WIDTH // RW_HEAD
DECAY_LORA = 96
AAA_LORA = 96
GATE_LORA = 256
RW_PROJ = 3 * RW_WIDTH + DECAY_LORA + AAA_LORA + GATE_LORA
LORA_PAD = 128
LORA_BLOCK = 2 * LORA_PAD + GATE_LORA
FEAT_PAD = 3 * RW_WIDTH + LORA_BLOCK
PROJ_MAIN = 3 * DA_WIDTH + 3 * RW_WIDTH
MEM_LEN = 256
MEM_HEADS = 4
MEM_DH = 128
MEM_WIDTH = MEM_HEADS * MEM_DH
PAGE_SIZE = 128
DEPTH = 1
LN_EPS = 1e-5
SUBLN_EPS = 1e-5
RW_GN_EPS = 64e-5
ALPHA = (2.0 * DEPTH) ** 0.25

LANES = 128
VMEM_LIMIT = 56 * 1024 * 1024
NEG_INF = float("-inf")

_NT = (((1,), (1,)), ((), ()))
LOG2E = math.log2(math.e)
QK_SCALE_LOG2 = DA_DH ** -0.5 * LOG2E


def _cparams(*sem):
    return pltpu.CompilerParams(dimension_semantics=sem, vmem_limit_bytes=VMEM_LIMIT)


def _layernorm(x, g, b):
    mu = jnp.mean(x, axis=-1, keepdims=True)
    d = x - mu
    var = jnp.mean(d * d, axis=-1, keepdims=True)
    return d * lax.rsqrt(var + LN_EPS) * g + b


def _sigmoid(x):
    return 1.0 / (1.0 + jnp.exp(-x))


def _split2(x):
    hi = x.astype(BF16)
    lo = (x - hi.astype(F32)).astype(BF16)
    return hi, lo


def _segsum(x, jj_ref):
    hi, lo = _split2(x)
    return jnp.dot(jnp.concatenate([hi, lo], axis=1), jj_ref[...], preferred_element_type=F32)


def _seg_ones(parts):
    r = np.arange(parts * LANES)[:, None] % LANES
    c = np.arange(LANES)[None, :]
    return jnp.asarray((r // RW_HEAD) == (c // RW_HEAD), dtype=BF16)


def _diag_mask():
    r = np.arange(RW_HEAD)[:, None]
    c = np.arange(LANES)[None, :]
    return jnp.asarray((c % RW_HEAD) == r, dtype=F32)


def _mm_kernel(a_ref, w_ref, o_ref):
    o_ref[...] = jnp.dot(a_ref[...], w_ref[...], preferred_element_type=F32)


def _matmul(a, w, tm, tn):
    m, k = a.shape
    n = w.shape[1]
    assert m % tm == 0 and n % tn == 0
    return pl.pallas_call(
        _mm_kernel,
        grid=(m // tm, n // tn),
        in_specs=[pl.BlockSpec((tm, k), lambda i, j: (i, 0)),
                  pl.BlockSpec((k, tn), lambda i, j: (0, j))],
        out_specs=pl.BlockSpec((tm, tn), lambda i, j: (i, j)),
        out_shape=jax.ShapeDtypeStruct((m, n), F32),
        compiler_params=_cparams("parallel", "arbitrary"),
        name="matmul",
    )(a, w)


def _lambda(lamv_ref, lam_init):
    lv = lamv_ref[...]
    s1 = jnp.sum(lv[0:1] * lv[1:2], axis=-1, keepdims=True)
    s2 = jnp.sum(lv[2:3] * lv[3:4], axis=-1, keepdims=True)
    return jnp.exp(s1) - jnp.exp(s2) + lam_init


def _diff_combine(acc_ref, l_ref, i1, i2, lam, g, lam_init):
    o = acc_ref[i1] / l_ref[i1] - lam * (acc_ref[i2] / l_ref[i2])
    o = o * lax.rsqrt(jnp.mean(o * o, axis=-1, keepdims=True) + SUBLN_EPS) * g
    return o * (1.0 - lam_init)


def _pattn_kernel(qi_ref, ki_ref, slopes_ref, lamv_ref, g_ref, q_ref, k_ref, v_ref, o_ref,
                  qs_scr, m_scr, l_scr, acc_scr, *, lam_init, tq, rc):
    h = pl.program_id(0)
    st = pl.program_id(1)
    qb = qi_ref[st]
    kb = ki_ref[st]
    slope = slopes_ref[h]

    @pl.when(kb == 0)
    def _init():
        qs_scr[...] = (q_ref[...] * QK_SCALE_LOG2).astype(BF16)
        m_scr[...] = jnp.full(m_scr.shape, NEG_INF, F32)
        l_scr[...] = jnp.zeros(l_scr.shape, F32)
        acc_scr[...] = jnp.zeros(acc_scr.shape, F32)

    def _step(masked):
        k = k_ref[...].astype(BF16)
        v = v_ref[...].astype(BF16)
        col = lax.broadcasted_iota(jnp.int32, (1, tq), 1)
        bias = slope * ((kb - qb) * tq + col).astype(F32)
        pieces = [(c, r0) for r0 in range(0, tq, rc) for c in range(2)]

        def scores(c, r0):
            s = lax.dot_general(qs_scr[r0:r0 + rc, c * DA_DH:(c + 1) * DA_DH], k[:, c * DA_DH:(c + 1) * DA_DH],
                                _NT, preferred_element_type=F32) + bias
            if masked:
                row2 = r0 + lax.broadcasted_iota(jnp.int32, (rc, tq), 0)
                col2 = lax.broadcasted_iota(jnp.int32, (rc, tq), 1)
                s = jnp.where(col2 > row2, NEG_INF, s)
            return s

        def softmax(c, r0, s):
            m_old = m_scr[c, r0:r0 + rc]
            m_new = jnp.maximum(m_old, jnp.max(s, axis=-1, keepdims=True))
            corr = jnp.exp2(m_old - m_new)
            p = jnp.exp2(s - m_new)
            l_scr[c, r0:r0 + rc] = l_scr[c, r0:r0 + rc] * corr + jnp.sum(p, axis=-1, keepdims=True)
            m_scr[c, r0:r0 + rc] = m_new
            return p.astype(BF16), corr

        def values(c, r0, p, corr):
            acc_scr[c, r0:r0 + rc] = acc_scr[c, r0:r0 + rc] * corr + jnp.dot(p, v, preferred_element_type=F32)

        n = len(pieces)
        s_q, p_q = {}, {}
        for i in range(n + 2):
            if i < n:
                s_q[i] = scores(*pieces[i])
            if 1 <= i <= n:
                p_q[i - 1] = softmax(*pieces[i - 1], s_q.pop(i - 1))
            if i >= 2:
                values(*pieces[i - 2], *p_q.pop(i - 2))

    @pl.when(kb < qb)
    def _off_diag():
        _step(False)

    @pl.when(kb == qb)
    def _diag():
        _step(True)
        lam = _lambda(lamv_ref, lam_init)
        o_ref[...] = _diff_combine(acc_scr, l_scr, 0, 1, lam, g_ref[...], lam_init).astype(o_ref.dtype)


def _prompt_attention(proj, lamv, subln_g, lam_init, tq, rc):
    t = proj.shape[0]
    assert t % tq == 0 and tq % rc == 0
    nq = t // tq
    qi = np.concatenate([np.full(q + 1, q, np.int32) for q in range(nq)])
    ki = np.concatenate([np.arange(q + 1, dtype=np.int32) for q in range(nq)])
    slopes = jnp.asarray(LOG2E * 2.0 ** (-8.0 * np.arange(1, DA_HEADS + 1) / DA_HEADS), F32)
    grid_spec = pltpu.PrefetchScalarGridSpec(
        num_scalar_prefetch=3,
        grid=(DA_HEADS, len(qi)),
        in_specs=[
            pl.BlockSpec((4, DA_DH), lambda h, s, qi, ki, sl: (0, 0)),
            pl.BlockSpec((1, DA_HV), lambda h, s, qi, ki, sl: (0, 0)),
            pl.BlockSpec((tq, DA_HV), lambda h, s, qi, ki, sl: (qi[s], h)),
            pl.BlockSpec((tq, DA_HV), lambda h, s, qi, ki, sl: (ki[s], DA_HEADS + h)),
            pl.BlockSpec((tq, DA_HV), lambda h, s, qi, ki, sl: (ki[s], 2 * DA_HEADS + h)),
        ],
        out_specs=pl.BlockSpec((tq, DA_HV), lambda h, s, qi, ki, sl: (qi[s], h)),
        scratch_shapes=[
            pltpu.VMEM((tq, DA_HV), BF16),
            pltpu.VMEM((2, tq, 1), F32),
            pltpu.VMEM((2, tq, 1), F32),
            pltpu.VMEM((2, tq, DA_HV), F32),
        ],
    )
    return pl.pallas_call(
        functools.partial(_pattn_kernel, lam_init=lam_init, tq=tq, rc=rc),
        grid_spec=grid_spec,
        out_shape=jax.ShapeDtypeStruct((t, DA_WIDTH), BF16),
        compiler_params=_cparams("parallel", "arbitrary"),
        name="prompt_attention",
    )(jnp.asarray(qi), jnp.asarray(ki), slopes, lamv, subln_g, proj, proj, proj)


def _sattn_kernel(pt_ref, lamv_ref, g_ref, q_ref, kn_ref, vn_ref, *rest, pages_per_step, lam_init):
    pp = pages_per_step
    kp_refs = rest[:pp]
    vp_refs = rest[pp:2 * pp]
    o_ref = rest[2 * pp]
    qs_scr, bias_scr, m_scr, l_scr, acc_scr = rest[2 * pp + 1:]
    st = pl.program_id(1)
    n_steps = pl.num_programs(1)
    rows = q_ref.shape[0]
    cols = PAGE_SIZE * DA_HEADS
    past_len = n_steps * pp * PAGE_SIZE

    head_r = lax.broadcasted_iota(jnp.int32, (rows, 1), 0) % DA_HEADS
    slope = LOG2E * jnp.exp2((head_r + 1).astype(F32) * (-8.0 / DA_HEADS))

    @pl.when(st == 0)
    def _init():
        qs_scr[...] = (q_ref[...] * QK_SCALE_LOG2).astype(BF16)
        lane = lax.broadcasted_iota(jnp.int32, (rows, cols), 1)
        row = lax.broadcasted_iota(jnp.int32, (rows, cols), 0)
        same_head = (lane % DA_HEADS) == (row % DA_HEADS)
        bias_scr[...] = jnp.where(same_head, slope * (lane // DA_HEADS).astype(F32), NEG_INF)
        m_scr[...] = jnp.full(m_scr.shape, NEG_INF, F32)
        l_scr[...] = jnp.zeros(l_scr.shape, F32)
        acc_scr[...] = jnp.zeros(acc_scr.shape, F32)

    def _block(k_ref, v_ref, bias):
        k = k_ref[...].astype(BF16)
        v = v_ref[...].astype(BF16)
        ps, corrs = [], []
        for c in range(2):
            s = lax.dot_general(qs_scr[:, c * DA_DH:(c + 1) * DA_DH], k[:, c * DA_DH:(c + 1) * DA_DH], _NT,
                                preferred_element_type=F32) + bias
            m_old = m_scr[c]
            m_new = jnp.maximum(m_old, jnp.max(s, axis=-1, keepdims=True))
            corr = jnp.exp2(m_old - m_new)
            p = jnp.exp2(s - m_new)
            l_scr[c] = l_scr[c] * corr + jnp.sum(p, axis=-1, keepdims=True)
            m_scr[c] = m_new
            ps.append(p.astype(BF16))
            corrs.append(corr)
        pv = jnp.dot(jnp.concatenate(ps, axis=0), v, preferred_element_type=F32)
        for c in range(2):
            acc_scr[c] = acc_scr[c] * corrs[c] + pv[c * rows:(c + 1) * rows]

    for p in range(pp):
        offset = ((st * pp + p) * PAGE_SIZE - past_len).astype(F32)
        _block(kp_refs[p], vp_refs[p], bias_scr[...] + slope * offset)

    @pl.when(st == n_steps - 1)
    def _last():
        lane = lax.broadcasted_iota(jnp.int32, (rows, cols), 1)
        row = lax.broadcasted_iota(jnp.int32, (rows, cols), 0)
        causal = (lane // DA_HEADS) <= (row // DA_HEADS)
        _block(kn_ref, vn_ref, jnp.where(causal, bias_scr[...], NEG_INF))
        lam = _lambda(lamv_ref, lam_init)
        o_ref[...] = _diff_combine(acc_scr, l_scr, 0, 1, lam, g_ref[...], lam_init).astype(o_ref.dtype)


def _sample_attention(proj, cache_k, cache_v, page_table, lamv, subln_g, lam_init, n_tok, pages_per_step):
    b, n_pages = page_table.shape
    pp = pages_per_step
    assert n_pages % pp == 0
    n_phys = cache_k.shape[0]
    rows = n_tok * DA_HEADS
    cols = PAGE_SIZE * DA_HEADS
    ck = cache_k.reshape(n_phys, cols, DA_HV)
    cv = cache_v.reshape(n_phys, cols, DA_HV)
    proj3 = proj.reshape(b, n_tok, proj.shape[1])
    q = proj3[:, :, :DA_WIDTH].reshape(b, rows, DA_HV)
    pad = ((0, 0), (0, PAGE_SIZE - n_tok), (0, 0))
    k_new = jnp.pad(proj3[:, :, DA_WIDTH:2 * DA_WIDTH], pad).reshape(b, cols, DA_HV)
    v_new = jnp.pad(proj3[:, :, 2 * DA_WIDTH:3 * DA_WIDTH], pad).reshape(b, cols, DA_HV)

    def page_spec(p):
        return pl.BlockSpec((None, cols, DA_HV), lambda bi, s, pt, p=p: (pt[bi * n_pages + s * pp + p], 0, 0))

    grid_spec = pltpu.PrefetchScalarGridSpec(
        num_scalar_prefetch=1,
        grid=(b, n_pages // pp),
        in_specs=[
            pl.BlockSpec((4, DA_DH), lambda bi, s, pt: (0, 0)),
            pl.BlockSpec((1, DA_HV), lambda bi, s, pt: (0, 0)),
            pl.BlockSpec((None, rows, DA_HV), lambda bi, s, pt: (bi, 0, 0)),
            pl.BlockSpec((None, cols, DA_HV), lambda bi, s, pt: (bi, 0, 0)),
            pl.BlockSpec((None, cols, DA_HV), lambda bi, s, pt: (bi, 0, 0)),
        ] + [page_spec(p) for p in range(pp)] + [page_spec(p) for p in range(pp)],
        out_specs=pl.BlockSpec((None, rows, DA_HV), lambda bi, s, pt: (bi, 0, 0)),
        scratch_shapes=[
            pltpu.VMEM((rows, DA_HV), BF16),
            pltpu.VMEM((rows, cols), F32),
            pltpu.VMEM((2, rows, 1), F32),
            pltpu.VMEM((2, rows, 1), F32),
            pltpu.VMEM((2, rows, DA_HV), F32),
        ],
    )
    out = pl.pallas_call(
        functools.partial(_sattn_kernel, pages_per_step=pp, lam_init=lam_init),
        grid_spec=grid_spec,
        out_shape=jax.ShapeDtypeStruct((b, rows, DA_HV), F32),
        compiler_params=_cparams("parallel", "arbitrary"),
        name="sample_attention",
    )(page_table.reshape(-1), lamv, subln_g, q, k_new, v_new, *([ck] * pp), *([cv] * pp))
    return out.reshape(b * n_tok, DA_WIDTH)


def _rwkv_prep_kernel(fr_ref, fk_ref, fv_ref, fl_ref, pr_ref, pk_ref, pv_ref, pl_ref,
                      mur_ref, muk_ref, muv_ref, mul_ref, w0_ref, a0_ref, kkw_ref, kaw_ref,
                      w2_ref, a2_ref, g2_ref, jj_ref,
                      r_out, w_out, k_out, v_out, kk_out, b_out, g_out,
                      cr_scr, ck_scr, cv_scr, cl_scr):
    tb = pl.program_id(1)
    tm = fr_ref.shape[0]

    @pl.when(tb == 0)
    def _load_state():
        cr_scr[...] = pr_ref[...]
        ck_scr[...] = pk_ref[...]
        cv_scr[...] = pv_ref[...]
        cl_scr[...] = pl_ref[...]

    def mixed(f_ref, carry_scr, mu_ref):
        x = f_ref[...]
        row = lax.broadcasted_iota(jnp.int32, x.shape, 0)
        shifted = jnp.where(row == 0, carry_scr[...], pltpu.roll(x, 1, 0))
        carry_scr[...] = x[tm - 1:tm, :]
        return x + (shifted - x) * mu_ref[...]

    xr = mixed(fr_ref, cr_scr, mur_ref)
    xk = mixed(fk_ref, ck_scr, muk_ref)
    xv = mixed(fv_ref, cv_scr, muv_ref)
    xl = mixed(fl_ref, cl_scr, mul_ref)

    dw = jnp.tanh(xl[:, 0:LORA_PAD]).astype(BF16)
    da = xl[:, LORA_PAD:2 * LORA_PAD].astype(BF16)
    dg = _sigmoid(xl[:, 2 * LORA_PAD:]).astype(BF16)
    wl = w0_ref[...] + jnp.dot(dw, w2_ref[...], preferred_element_type=F32)
    neg = -wl
    softplus = jnp.maximum(neg, 0.0) + jnp.log(1.0 + jnp.exp(-jnp.abs(neg)))
    log_w = -jnp.exp(-softplus - 0.5)
    a = _sigmoid(a0_ref[...] + jnp.dot(da, a2_ref[...], preferred_element_type=F32))
    g = jnp.dot(dg, g2_ref[...], preferred_element_type=F32)

    kk0 = xk * kkw_ref[...]
    sq = kk0 * kk0
    ss = jnp.concatenate([_segsum(sq[:, n * LANES:(n + 1) * LANES], jj_ref)
                          for n in range(RW_WIDTH // LANES)], axis=1)
    kk = kk0 / jnp.maximum(jnp.sqrt(ss), 1e-12)

    r_out[...] = xr
    w_out[...] = log_w
    k_out[...] = xk * (1.0 + (a - 1.0) * kaw_ref[...])
    v_out[...] = xv
    kk_out[...] = kk
    b_out[...] = kk * a
    g_out[...] = g


def _rwkv_prep(proj, lora, prev_pad, mu_pad, w0, a0, k_k, k_a, w2p, a2p, g2b, n_batch, tm):
    m = proj.shape[0]
    t = m // n_batch
    assert t % tm == 0
    nt = t // tm
    base = 3 * DA_WIDTH // RW_WIDTH
    prev3 = prev_pad.reshape(n_batch, 1, FEAT_PAD)
    mu2 = mu_pad.reshape(1, FEAT_PAD)

    def feat(sec):
        return pl.BlockSpec((tm, RW_WIDTH), lambda b, i, sec=sec: (b * nt + i, base + sec))

    def prev(sec):
        return pl.BlockSpec((None, 1, RW_WIDTH), lambda b, i, sec=sec: (b, 0, sec))

    def mu(sec):
        return pl.BlockSpec((1, RW_WIDTH), lambda b, i, sec=sec: (0, sec))

    vec = pl.BlockSpec((1, RW_WIDTH), lambda b, i: (0, 0))
    out = pl.BlockSpec((tm, RW_WIDTH), lambda b, i: (b * nt + i, 0))
    in_specs = [
        feat(0), feat(1), feat(2),
        pl.BlockSpec((tm, LORA_BLOCK), lambda b, i: (b * nt + i, 0)),
        prev(0), prev(1), prev(2),
        pl.BlockSpec((None, 1, LORA_BLOCK), lambda b, i: (b, 0, 3 * RW_WIDTH // LORA_BLOCK)),
        mu(0), mu(1), mu(2),
        pl.BlockSpec((1, LORA_BLOCK), lambda b, i: (0, 3 * RW_WIDTH // LORA_BLOCK)),
        vec, vec, vec, vec,
        pl.BlockSpec((LORA_PAD, RW_WIDTH), lambda b, i: (0, 0)),
        pl.BlockSpec((LORA_PAD, RW_WIDTH), lambda b, i: (0, 0)),
        pl.BlockSpec((GATE_LORA, RW_WIDTH), lambda b, i: (0, 0)),
        pl.BlockSpec((2 * LANES, LANES), lambda b, i: (0, 0)),
    ]
    shp = jax.ShapeDtypeStruct((m, RW_WIDTH), F32)
    return pl.pallas_call(
        _rwkv_prep_kernel,
        grid=(n_batch, nt),
        in_specs=in_specs,
        out_specs=[out] * 7,
        out_shape=[shp] * 7,
        scratch_shapes=[pltpu.VMEM((1, RW_WIDTH), F32)] * 3 + [pltpu.VMEM((1, LORA_BLOCK), F32)],
        compiler_params=_cparams("parallel", "arbitrary"),
        name="rwkv_prep",
    )(proj, proj, proj, lora, prev3, prev3, prev3, prev3, mu2, mu2, mu2, mu2,
      w0.reshape(1, -1), a0.reshape(1, -1), k_k.reshape(1, -1), k_a.reshape(1, -1),
      w2p, a2p, g2b, _seg_ones(2))


N_TILES = RW_WIDTH // LANES
S_ROWS = N_TILES * RW_HEAD


def _wkv_kernel(r_ref, w_ref, k_ref, v_ref, kk_ref, b_ref, g_ref, s0_ref,
                lng_ref, lnb_ref, rk_ref, jj2_ref, jj3_ref, dm_ref,
                o_ref, st_ref, s_scr, o_scr):
    tb = pl.program_id(1)
    n_tb = pl.num_programs(1)
    t_blk = r_ref.shape[0]

    @pl.when(tb == 0)
    def _load_state():
        s_scr[...] = s0_ref[...]

    dm = dm_ref[...]
    dm_all = jnp.concatenate([dm] * N_TILES, axis=0)
    dm_all_b = dm_all.astype(BF16)

    def expand(row):
        return jnp.concatenate(
            [jnp.broadcast_to(row[:, n * LANES:(n + 1) * LANES], (RW_HEAD, LANES)) for n in range(N_TILES)],
            axis=0)

    def step(t, carry):
        s = s_scr[...]
        kkx = expand(kk_ref[pl.ds(t, 1), :])
        wx = expand(jnp.exp(w_ref[pl.ds(t, 1), :]))
        bx = expand(b_ref[pl.ds(t, 1), :])
        kx = expand(k_ref[pl.ds(t, 1), :])
        rx = expand(r_ref[pl.ds(t, 1), :])
        sa = -_segsum(s * kkx, jj2_ref)
        v_row = v_ref[pl.ds(t, 1), :]
        v_hi = v_row.astype(BF16)
        rem = v_row - v_hi.astype(F32)
        v_mid = rem.astype(BF16)
        v_lo = (rem - v_mid.astype(F32)).astype(BF16)
        v_parts = jnp.concatenate([expand(p) * dm_all_b for p in (v_hi, v_mid, v_lo)], axis=1)
        vx = jnp.dot(v_parts, jj3_ref[...], preferred_element_type=F32)
        s_new = s * wx + sa * bx + vx * kx
        s_scr[...] = s_new
        ox = _segsum(s_new * rx, jj2_ref) * dm_all
        o_row = jnp.concatenate(
            [jnp.sum(ox[n * RW_HEAD:(n + 1) * RW_HEAD], axis=0, keepdims=True) for n in range(N_TILES)], axis=1)
        o_scr[pl.ds(t, 1), :] = o_row
        return carry

    lax.fori_loop(0, t_blk, step, 0)

    for n in range(N_TILES):
        sl = slice(n * LANES, (n + 1) * LANES)
        o = o_scr[:, sl]
        mu = _segsum(o, jj2_ref) * (1.0 / RW_HEAD)
        d = o - mu
        var = _segsum(d * d, jj2_ref) * (1.0 / RW_HEAD)
        on = d * lax.rsqrt(var + RW_GN_EPS) * lng_ref[:, sl] + lnb_ref[:, sl]
        bonus = _segsum(r_ref[:, sl] * k_ref[:, sl] * rk_ref[:, sl], jj2_ref) * v_ref[:, sl]
        o_ref[:, sl] = ((on + bonus) * g_ref[:, sl]).astype(o_ref.dtype)

    @pl.when(tb == n_tb - 1)
    def _store_state():
        st_ref[...] = s_scr[...]


def _state_to_tiles(s):
    b = s.shape[0]
    return s.reshape(b, N_TILES, 2, RW_HEAD, RW_HEAD).transpose(0, 1, 3, 2, 4).reshape(b, S_ROWS, LANES)


def _tiles_to_state(s):
    b = s.shape[0]
    return s.reshape(b, N_TILES, RW_HEAD, 2, RW_HEAD).transpose(0, 1, 3, 2, 4).reshape(
        b, RW_HEADS, RW_HEAD, RW_HEAD)


def _wkv(feats, s0_tiles, lnx_g, lnx_b, r_k, n_batch, t_blk):
    r, w, k, v, kk, bb, g = feats
    m = r.shape[0]
    t = m // n_batch
    assert t % t_blk == 0
    nt = t // t_blk
    blk = pl.BlockSpec((t_blk, RW_WIDTH), lambda b, i: (b * nt + i, 0))
    vec = pl.BlockSpec((1, RW_WIDTH), lambda b, i: (0, 0))
    st = pl.BlockSpec((None, S_ROWS, LANES), lambda b, i: (b, 0, 0))
    return pl.pallas_call(
        _wkv_kernel,
        grid=(n_batch, nt),
        in_specs=[blk] * 7 + [st, vec, vec, vec,
                              pl.BlockSpec((2 * LANES, LANES), lambda b, i: (0, 0)),
                              pl.BlockSpec((3 * LANES, LANES), lambda b, i: (0, 0)),
                              pl.BlockSpec((RW_HEAD, LANES), lambda b, i: (0, 0))],
        out_specs=[blk, st],
        out_shape=[jax.ShapeDtypeStruct((m, RW_WIDTH), BF16),
                   jax.ShapeDtypeStruct((n_batch, S_ROWS, LANES), F32)],
        scratch_shapes=[pltpu.VMEM((S_ROWS, LANES), F32), pltpu.VMEM((t_blk, RW_WIDTH), F32)],
        compiler_params=_cparams("parallel", "arbitrary"),
        name="wkv",
    )(r, w, k, v, kk, bb, g, s0_tiles, lnx_g.reshape(1, -1), lnx_b.reshape(1, -1), r_k.reshape(1, -1),
      _seg_ones(2), _seg_ones(3), _diag_mask())


_NN = (((1,), (0,)), ((), ()))


def _mm3(a, b, dims=_NN):
    f = lambda x, y: lax.dot_general(x, y, dims, preferred_element_type=F32)
    return f(a[0], b[0]) + (f(a[0], b[1]) + f(a[1], b[0]))


def _wkv_chunk_tiles(lw, r, k, v, kk, b, ys_in, c_len):
    rows = 2 * c_len
    lane = lax.broadcasted_iota(jnp.int32, (1, LANES), 1)
    head_a = (lane < RW_HEAD).astype(F32)
    head_b = 1.0 - head_a

    def stack(x):
        return jnp.concatenate([x * head_a, x * head_b], axis=0)

    def each(fn, *lists):
        return [fn(*args) for args in zip(*lists)]

    tr = lax.broadcasted_iota(jnp.int32, (c_len, c_len), 0)
    tc = lax.broadcasted_iota(jnp.int32, (c_len, c_len), 1)
    tril = (tc <= tr).astype(BF16)

    def cumsum(x):
        l1 = x.astype(BF16)
        rem = x - l1.astype(F32)
        l2 = rem.astype(BF16)
        l3 = (rem - l2.astype(F32)).astype(BF16)
        return (jnp.dot(tril, l1, preferred_element_type=F32) + jnp.dot(tril, l2, preferred_element_type=F32)
                + jnp.dot(tril, l3, preferred_element_type=F32))

    cs = each(cumsum, lw)
    cs_last = [c[c_len - 1:c_len, :] for c in cs]
    g_rem = each(lambda c, cl: jnp.exp(cl - c), cs, cs_last)
    g_inv = each(lambda c: jnp.exp(-c), cs)
    a2f = each(lambda x, c, l: stack(-x * jnp.exp(c - l)), kk, cs, lw)
    r2f = each(lambda x, c: stack(x * jnp.exp(c)), r, cs)
    v2f = each(stack, v)
    a2 = each(_split2, a2f)
    r2 = each(_split2, r2f)
    v2 = each(_split2, v2f)
    b2 = each(lambda x, g: _split2(stack(x * g)), b, g_inv)
    k2 = each(lambda x, g: _split2(stack(x * g)), k, g_inv)
    bg2 = each(lambda x, g: _split2(stack(x * g)), b, g_rem)
    kg2 = each(lambda x, g: _split2(stack(x * g)), k, g_rem)

    rr = lax.broadcasted_iota(jnp.int32, (rows, rows), 0)
    cc = lax.broadcasted_iota(jnp.int32, (rows, rows), 1)
    same = (rr // c_len) == (cc // c_len)
    strict = same & ((cc % c_len) < (rr % c_len))
    lower = same & ((cc % c_len) <= (rr % c_len))
    eye = jnp.where(rr == cc, 1.0, 0.0)

    ar2 = each(lambda x, y: (jnp.concatenate([x[0], y[0]], axis=0), jnp.concatenate([x[1], y[1]], axis=0)), a2, r2)
    gb = each(lambda x, y: _mm3(x, y, _NT), ar2, b2)
    gk = each(lambda x, y: _mm3(x, y, _NT), ar2, k2)
    l_mat = [jnp.where(strict, x[:rows], 0.0) for x in gb]
    ak = [jnp.where(strict, x[:rows], 0.0) for x in gk]
    rb = [jnp.where(lower, x[rows:], 0.0) for x in gb]
    rk = [jnp.where(lower, x[rows:], 0.0) for x in gk]

    t_mat = [eye + x for x in l_mat]
    p_mat = l_mat
    for _ in range(max(int(math.log2(c_len)) - 1, 0)):
        ps = each(_split2, p_mat)
        p_mat = each(lambda x: _mm3(x, x), ps)
        t_mat = each(lambda t, p: t + _mm3(_split2(t), _split2(p)), t_mat, p_mat)

    akv = each(lambda x, y: _mm3(_split2(x), y), ak, v2)
    wu = each(lambda t, x, y: _mm3(_split2(t), _split2(jnp.concatenate([x, y], axis=1))),
              t_mat, a2f, akv)
    one = lambda x, y, dims=_NN: lax.dot_general(x.astype(BF16), y.astype(BF16), dims, preferred_element_type=F32)
    qo = each(one, rb, wu)
    q2 = each(lambda x, y: x + y[:, :LANES], r2f, qo)
    o0 = each(lambda q, x, y: q[:, LANES:] + one(x, y), qo, rk, v2f)
    mn = each(lambda x, y: _mm3(_split2(x.T), y), wu, bg2)
    n_mat = each(lambda m, x, y: m[LANES:] + _mm3(_split2(x.T), y), mn, v2f, kg2)

    ys = each(_split2, ys_in)
    o2 = each(lambda q, y, o: one(q, y, _NT) + o, q2, ys_in, o0)
    o = [x[:c_len] + x[c_len:] for x in o2]
    y_new = each(lambda y, cl, ysp, m, nm: y * jnp.exp(cl) + _mm3(ysp, _split2(m[:LANES])) + nm,
                 ys_in, cs_last, ys, mn, n_mat)
    return o, y_new


def _wkv_chunk_kernel(lw_ref, r_ref, k_ref, v_ref, kk_ref, b_ref, g_ref, y0_ref,
                      lng_ref, lnb_ref, rk_ref, jj2_ref, o_ref, yt_ref, y_scr):
    c = pl.program_id(2)
    n_c = pl.num_programs(2)
    c_len = r_ref.shape[0]
    tiles = r_ref.shape[1] // LANES

    @pl.when(c == 0)
    def _load_state():
        y_scr[...] = y0_ref[...]

    sls = [slice(t * LANES, (t + 1) * LANES) for t in range(tiles)]
    split = lambda ref: [ref[:, sl] for sl in sls]
    r, k, v = split(r_ref), split(k_ref), split(v_ref)
    o, y_new = _wkv_chunk_tiles(split(lw_ref), r, k, v, split(kk_ref), split(b_ref),
                                [y_scr[t] for t in range(tiles)], c_len)
    for t, sl in enumerate(sls):
        y_scr[t] = y_new[t]
        mu = _segsum(o[t], jj2_ref) * (1.0 / RW_HEAD)
        d = o[t] - mu
        var = _segsum(d * d, jj2_ref) * (1.0 / RW_HEAD)
        on = d * lax.rsqrt(var + RW_GN_EPS) * lng_ref[:, sl] + lnb_ref[:, sl]
        bonus = _segsum(r[t] * k[t] * rk_ref[:, sl], jj2_ref) * v[t]
        o_ref[:, sl] = ((on + bonus) * g_ref[:, sl]).astype(o_ref.dtype)

    @pl.when(c == n_c - 1)
    def _store_state():
        yt_ref[...] = y_scr[...]


def _state_to_blockdiag(s):
    b = s.shape[0]
    s5 = s.reshape(b, N_TILES, 2, RW_HEAD, RW_HEAD)
    z = jnp.zeros((b, N_TILES, RW_HEAD, RW_HEAD), s.dtype)
    top = jnp.concatenate([s5[:, :, 0], z], axis=-1)
    bot = jnp.concatenate([z, s5[:, :, 1]], axis=-1)
    return jnp.concatenate([top, bot], axis=-2)


def _blockdiag_to_state(y):
    b = y.shape[0]
    s = jnp.stack([y[:, :, :RW_HEAD, :RW_HEAD], y[:, :, RW_HEAD:, RW_HEAD:]], axis=2)
    return s.reshape(b, RW_HEADS, RW_HEAD, RW_HEAD)


def _wkv_chunked(feats, y0, lnx_g, lnx_b, r_k, n_batch, c_len, tiles):
    r, lw, k, v, kk, bb, g = feats
    m = r.shape[0]
    t = m // n_batch
    assert t % c_len == 0 and N_TILES % tiles == 0
    nc = t // c_len
    wid = tiles * LANES
    blk = pl.BlockSpec((c_len, wid), lambda b, gi, c: (b * nc + c, gi))
    vec = pl.BlockSpec((1, wid), lambda b, gi, c: (0, gi))
    st = pl.BlockSpec((None, tiles, LANES, LANES), lambda b, gi, c: (b, gi, 0, 0))
    return pl.pallas_call(
        _wkv_chunk_kernel,
        grid=(n_batch, N_TILES // tiles, nc),
        in_specs=[blk] * 7 + [st, vec, vec, vec, pl.BlockSpec((2 * LANES, LANES), lambda b, gi, c: (0, 0))],
        out_specs=[blk, st],
        out_shape=[jax.ShapeDtypeStruct((m, RW_WIDTH), BF16),
                   jax.ShapeDtypeStruct((n_batch, N_TILES, LANES, LANES), F32)],
        scratch_shapes=[pltpu.VMEM((tiles, LANES, LANES), F32)],
        compiler_params=_cparams("parallel", "parallel", "arbitrary"),
        name="wkv_chunked",
    )(lw, r, k, v, kk, bb, g, y0, lnx_g.reshape(1, -1), lnx_b.reshape(1, -1), r_k.reshape(1, -1), _seg_ones(2))


def _outproj_ln_kernel(a1_ref, a2_ref, w_ref, res_ref, g_ref, b_ref, o_ref, acc_scr):
    j = pl.program_id(1)
    nn = pl.num_programs(1)
    k1 = a1_ref.shape[1]
    mix = (jnp.dot(a1_ref[...], w_ref[:k1, :], preferred_element_type=F32)
           + jnp.dot(a2_ref[...], w_ref[k1:, :], preferred_element_type=F32))
    acc_scr[j] = ALPHA * res_ref[...] + mix

    @pl.when(j == nn - 1)
    def _finish():
        y = jnp.concatenate([acc_scr[n] for n in range(acc_scr.shape[0])], axis=1)
        o_ref[...] = _layernorm(y, g_ref[...], b_ref[...])


def _outproj_ln(a1, a2, w, res, g, b, tm, tn):
    m, k1 = a1.shape
    k2 = a2.shape[1]
    n = w.shape[1]
    assert m % tm == 0 and n % tn == 0
    return pl.pallas_call(
        _outproj_ln_kernel,
        grid=(m // tm, n // tn),
        in_specs=[pl.BlockSpec((tm, k1), lambda i, j: (i, 0)),
                  pl.BlockSpec((tm, k2), lambda i, j: (i, 0)),
                  pl.BlockSpec((k1 + k2, tn), lambda i, j: (0, j)),
                  pl.BlockSpec((tm, tn), lambda i, j: (i, j)),
                  pl.BlockSpec((1, n), lambda i, j: (0, 0)),
                  pl.BlockSpec((1, n), lambda i, j: (0, 0))],
        out_specs=pl.BlockSpec((tm, n), lambda i, j: (i, 0)),
        out_shape=jax.ShapeDtypeStruct((m, n), F32),
        scratch_shapes=[pltpu.VMEM((n // tn, tm, tn), F32)],
        compiler_params=_cparams("parallel", "arbitrary"),
        name="outproj_ln1",
    )(a1, a2, w, res, g.reshape(1, -1), b.reshape(1, -1))


def _memattn_kernel(x_ref, mk_ref, mv_ref, wq_ref, wo_ref, g_ref, b_ref, o_ref):
    x = x_ref[...]
    q = jnp.dot(x.astype(BF16), wq_ref[...], preferred_element_type=F32) * (MEM_DH ** -0.5)
    mk = mk_ref[...].astype(BF16)
    mv = mv_ref[...].astype(BF16)
    heads = []
    for h in range(MEM_HEADS):
        sl = slice(h * MEM_DH, (h + 1) * MEM_DH)
        s = lax.dot_general(q[:, sl].astype(BF16), mk[:, sl], _NT, preferred_element_type=F32)
        p = jnp.exp(s - jnp.max(s, axis=-1, keepdims=True))
        l = jnp.sum(p, axis=-1, keepdims=True)
        heads.append(jnp.dot(p.astype(BF16), mv[:, sl], preferred_element_type=F32) / l)
    o = jnp.concatenate(heads, axis=1).astype(BF16)
    y = ALPHA * x + jnp.dot(o, wo_ref[...], preferred_element_type=F32)
    o_ref[...] = _layernorm(y, g_ref[...], b_ref[...])


def _memattn_ln(x, mk, mv, wq, wo, g, b, n_batch, tm):
    m, d = x.shape
    t = m // n_batch
    assert t % tm == 0
    nt = t // tm
    return pl.pallas_call(
        _memattn_kernel,
        grid=(n_batch, nt),
        in_specs=[pl.BlockSpec((tm, d), lambda bi, i: (bi * nt + i, 0)),
                  pl.BlockSpec((None, MEM_LEN, MEM_WIDTH), lambda bi, i: (bi, 0, 0)),
                  pl.BlockSpec((None, MEM_LEN, MEM_WIDTH), lambda bi, i: (bi, 0, 0)),
                  pl.BlockSpec((d, MEM_WIDTH), lambda bi, i: (0, 0)),
                  pl.BlockSpec((MEM_WIDTH, d), lambda bi, i: (0, 0)),
                  pl.BlockSpec((1, d), lambda bi, i: (0, 0)),
                  pl.BlockSpec((1, d), lambda bi, i: (0, 0))],
        out_specs=pl.BlockSpec((tm, d), lambda bi, i: (bi * nt + i, 0)),
        out_shape=jax.ShapeDtypeStruct((m, d), F32),
        compiler_params=_cparams("parallel", "arbitrary"),
        name="memattn_ln2",
    )(x, mk, mv, wq, wo, g.reshape(1, -1), b.reshape(1, -1))


def _swiglu_kernel(x_ref, wg_ref, wu_ref, wd_ref, g_ref, b_ref, o_ref, xb_scr):
    f = pl.program_id(1)
    nf = pl.num_programs(1)

    @pl.when(f == 0)
    def _init():
        x = x_ref[...]
        xb_scr[...] = x.astype(BF16)
        o_ref[...] = ALPHA * x

    xb = xb_scr[...]
    gate = jnp.dot(xb, wg_ref[...], preferred_element_type=F32)
    up = jnp.dot(xb, wu_ref[...], preferred_element_type=F32)
    hidden = (gate * _sigmoid(gate) * up).astype(BF16)
    o_ref[...] += jnp.dot(hidden, wd_ref[...], preferred_element_type=F32)

    @pl.when(f == nf - 1)
    def _finish():
        o_ref[...] = _layernorm(o_ref[...], g_ref[...], b_ref[...])


def _swiglu_ln(x, wg, wu, wd, g, b, tm, tf):
    m, d = x.shape
    f = wg.shape[1]
    assert m % tm == 0 and f % tf == 0
    return pl.pallas_call(
        _swiglu_kernel,
        grid=(m // tm, f // tf),
        in_specs=[pl.BlockSpec((tm, d), lambda i, j: (i, 0), pipeline_mode=pl.Buffered(1)),
                  pl.BlockSpec((d, tf), lambda i, j: (0, j)),
                  pl.BlockSpec((d, tf), lambda i, j: (0, j)),
                  pl.BlockSpec((tf, d), lambda i, j: (j, 0)),
                  pl.BlockSpec((1, d), lambda i, j: (0, 0)),
                  pl.BlockSpec((1, d), lambda i, j: (0, 0))],
        out_specs=pl.BlockSpec((tm, d), lambda i, j: (i, 0), pipeline_mode=pl.Buffered(1)),
        out_shape=jax.ShapeDtypeStruct((m, d), F32),
        scratch_shapes=[pltpu.VMEM((tm, d), BF16)],
        compiler_params=_cparams("parallel", "arbitrary"),
        name="swiglu_ln3",
    )(x, wg, wu, wd, g.reshape(1, -1), b.reshape(1, -1))


def _pad_feat(x):
    c = 3 * RW_WIDTH
    z = jnp.zeros(x.shape[:-1] + (LORA_PAD - DECAY_LORA,), x.dtype)
    return jnp.concatenate([x[..., :c + DECAY_LORA], z, x[..., c + DECAY_LORA:c + DECAY_LORA + AAA_LORA], z,
                            x[..., c + DECAY_LORA + AAA_LORA:]], axis=-1)


def _unpad_feat(x):
    c = 3 * RW_WIDTH
    return jnp.concatenate([x[..., :c + DECAY_LORA], x[..., c + LORA_PAD:c + LORA_PAD + AAA_LORA],
                            x[..., c + 2 * LORA_PAD:]], axis=-1)


def _pad_rows(w, rows):
    return jnp.concatenate([w, jnp.zeros((rows - w.shape[0],) + w.shape[1:], w.dtype)], axis=0)


def _layer(x2d, n_batch, attn_fn, wkv0, shift0_pad, mem_k, mem_v, p, cfg):
    xb = x2d.astype(BF16)
    proj = _matmul(xb, p["w_in"], cfg["mm_tm"], cfg["mm_tn"])
    lora = _matmul(xb, p["w_in_lora"], cfg["mm_tm"], LORA_BLOCK)
    o_a = attn_fn(proj)
    feats = _rwkv_prep(proj, lora, shift0_pad, p["mu_shift"], p["w0"], p["a0"], p["k_k"], p["k_a"],
                       p["w2"], p["a2"], p["g2"], n_batch, cfg["prep_tm"])
    if cfg["wkv_chunk"]:
        o_r, y_t = _wkv_chunked(feats, _state_to_blockdiag(wkv0), p["lnx_g"], p["lnx_b"], p["r_k"], n_batch,
                                cfg["wkv_chunk"], cfg["wkv_tiles"])
        wkv_t = _blockdiag_to_state(y_t)
    else:
        o_r, s_t = _wkv(feats, _state_to_tiles(wkv0), p["lnx_g"], p["lnx_b"], p["r_k"], n_batch, cfg["wkv_tb"])
        wkv_t = _tiles_to_state(s_t)
    x1 = _outproj_ln(o_a, o_r, p["w_out"], x2d, p["ln1_g"], p["ln1_b"], cfg["op_tm"], cfg["op_tn"])
    x2 = _memattn_ln(x1, mem_k, mem_v, p["wq_m"], p["wo_m"], p["ln2_g"], p["ln2_b"], n_batch, cfg["ma_tm"])
    y = _swiglu_ln(x2, p["w_gate"], p["w_up"], p["w_down"], p["ln3_g"], p["ln3_b"], cfg["ff_tm"], cfg["ff_tf"])
    return y, proj, lora, wkv_t


_PROMPT_CFG = dict(mm_tm=1024, mm_tn=512, prep_tm=256, wkv_chunk=64, wkv_tiles=8, op_tm=512, op_tn=512,
                   ma_tm=256, ff_tm=512, ff_tf=256, attn_tq=1024, attn_rc=512)
_SAMPLE_CFG = dict(mm_tm=256, mm_tn=512, prep_tm=8, wkv_chunk=0, wkv_tb=8, op_tm=256, op_tn=512, ma_tm=8,
                   ff_tm=256, ff_tf=256, pages_per_step=8)


def kernel(x_prompt, x_sample, cache_k, cache_v, cache_mem_k, cache_mem_v, state_wkv, state_shift, page_table, mem_prompt, w_in, lam_q1, lam_k1, lam_q2, lam_k2, subln_g, mu_shift, w0, w2, a0, a2, g2, k_k, k_a, r_k, lnx_g, lnx_b, w_out, ln1_g, ln1_b, wq_m, wk_m, wv_m, wo_m, ln2_g, ln2_b, w_gate, w_up, w_down, ln3_g, ln3_b):
    bp, tp, d = x_prompt.shape
    bs, ts, _ = x_sample.shape
    yp = x_prompt.reshape(bp * tp, d)
    ys = x_sample.reshape(bs * ts, d)
    outs = [[] for _ in range(10)]
    for l in range(DEPTH):
        lam_init = 0.8 - 0.6 * math.exp(-0.3 * l)
        c = 3 * DA_WIDTH + 3 * RW_WIDTH
        w_in_l = w_in[l]
        zc = jnp.zeros((d, LORA_PAD - DECAY_LORA), BF16)
        w_lora = w_in_l[:, c:].astype(BF16)
        w_in_lora = jnp.concatenate(
            [w_lora[:, :DECAY_LORA], zc, w_lora[:, DECAY_LORA:DECAY_LORA + AAA_LORA], zc,
             w_lora[:, DECAY_LORA + AAA_LORA:]], axis=1)
        p = dict(
            w_in=w_in_l[:, :c].astype(BF16), w_in_lora=w_in_lora, mu_shift=_pad_feat(mu_shift[l]), w0=w0[l], a0=a0[l], k_k=k_k[l], k_a=k_a[l],
            w2=_pad_rows(w2[l], LORA_PAD).astype(BF16), a2=_pad_rows(a2[l], LORA_PAD).astype(BF16),
            g2=g2[l].astype(BF16), r_k=r_k[l], lnx_g=lnx_g[l], lnx_b=lnx_b[l],
            w_out=w_out[l].astype(BF16), ln1_g=ln1_g[l], ln1_b=ln1_b[l],
            wq_m=wq_m[l].astype(BF16), wo_m=wo_m[l].astype(BF16), ln2_g=ln2_g[l], ln2_b=ln2_b[l],
            w_gate=w_gate[l].astype(BF16), w_up=w_up[l].astype(BF16), w_down=w_down[l].astype(BF16),
            ln3_g=ln3_g[l], ln3_b=ln3_b[l])
        lamv = jnp.stack([lam_q1[l], lam_k1[l], lam_q2[l], lam_k2[l]])
        g_sub = subln_g[l].reshape(1, DA_HV)

        w_kv = jnp.concatenate([wk_m[l], wv_m[l]], axis=1).astype(BF16)
        mem_kv = _matmul(mem_prompt.reshape(bp * MEM_LEN, d).astype(BF16), w_kv, MEM_LEN, MEM_WIDTH)
        mk_p = mem_kv[:, :MEM_WIDTH].reshape(bp, MEM_LEN, MEM_WIDTH)
        mv_p = mem_kv[:, MEM_WIDTH:].reshape(bp, MEM_LEN, MEM_WIDTH)
        attn_p = functools.partial(_prompt_attention, lamv=lamv, subln_g=g_sub, lam_init=lam_init,
                                   tq=_PROMPT_CFG["attn_tq"], rc=_PROMPT_CFG["attn_rc"])
        yp, proj_p, lora_p, wkv_p = _layer(yp, bp, attn_p, jnp.zeros((bp, RW_HEADS, RW_HEAD, RW_HEAD), F32),
                                   jnp.zeros((bp, FEAT_PAD), F32), mk_p, mv_p, p, _PROMPT_CFG)

        def attn_s(proj, l=l, lamv=lamv, g_sub=g_sub, lam_init=lam_init):
            o = _sample_attention(proj, cache_k[l], cache_v[l], page_table, lamv, g_sub, lam_init, ts,
                                  _SAMPLE_CFG["pages_per_step"])
            return o.astype(BF16)

        ys, proj_s, lora_s, wkv_s = _layer(ys, bs, attn_s, state_wkv[l], _pad_feat(state_shift[l]),
                                   cache_mem_k[l].reshape(bs, MEM_LEN, MEM_WIDTH),
                                   cache_mem_v[l].reshape(bs, MEM_LEN, MEM_WIDTH), p, _SAMPLE_CFG)

        pp3 = proj_p.reshape(bp, tp, PROJ_MAIN)
        ps3 = proj_s.reshape(bs, ts, PROJ_MAIN)
        last_p = jnp.concatenate([pp3[:, tp - 1, 3 * DA_WIDTH:], lora_p.reshape(bp, tp, LORA_BLOCK)[:, tp - 1]], -1)
        last_s = jnp.concatenate([ps3[:, ts - 1, 3 * DA_WIDTH:], lora_s.reshape(bs, ts, LORA_BLOCK)[:, ts - 1]], -1)
        outs[0].append(pp3[:, :, DA_WIDTH:2 * DA_WIDTH].reshape(bp, tp, DA_HEADS, DA_HV))
        outs[1].append(pp3[:, :, 2 * DA_WIDTH:3 * DA_WIDTH].reshape(bp, tp, DA_HEADS, DA_HV))
        outs[2].append(wkv_p)
        outs[3].append(_unpad_feat(last_p))
        outs[4].append(mk_p.reshape(bp, MEM_LEN, MEM_HEADS, MEM_DH))
        outs[5].append(mv_p.reshape(bp, MEM_LEN, MEM_HEADS, MEM_DH))
        outs[6].append(ps3[:, :, DA_WIDTH:2 * DA_WIDTH].reshape(bs, ts, DA_HEADS, DA_HV))
        outs[7].append(ps3[:, :, 2 * DA_WIDTH:3 * DA_WIDTH].reshape(bs, ts, DA_HEADS, DA_HV))
        outs[8].append(wkv_s)
        outs[9].append(_unpad_feat(last_s))
    return (yp.reshape(bp, tp, d), ys.reshape(bs, ts, d)) + tuple(jnp.stack(o) for o in outs)
```

```python
import functools
import math

import jax
import jax.numpy as jnp
import numpy as np
from jax import lax
from jax.experimental import pallas as pl
from jax.experimental.pallas import tpu as pltpu

F32 = jnp.float32
BF16 = jnp.bfloat16

D_MODEL = 4096
DA_HEADS = 8
DA_DH = 128
DA_HV = 2 * DA_DH
DA_WIDTH = DA_HEADS * DA_HV
RW_HEAD = 64
RW_WIDTH = 2048
RW_HEADS = RW_WIDTH // RW_HEAD
DECAY_LORA = 96
AAA_LORA = 96
GATE_LORA = 256
RW_PROJ = 3 * RW_WIDTH + DECAY_LORA + AAA_LORA + GATE_LORA
LORA_PAD = 128
LORA_BLOCK = 2 * LORA_PAD + GATE_LORA
FEAT_PAD = 3 * RW_WIDTH + LORA_BLOCK
MEM_LEN = 256
MEM_HEADS = 4
MEM_DH = 128
MEM_WIDTH = MEM_HEADS * MEM_DH
PAGE_SIZE = 128
DEPTH = 1
LN_EPS = 1e-5
SUBLN_EPS = 1e-5
RW_GN_EPS = 64e-5
ALPHA = (2.0 * DEPTH) ** 0.25

LANES = 128
VMEM_LIMIT = 56 * 1024 * 1024
NEG_INF = float("-inf")

_NT = (((1,), (1,)), ((), ()))
LOG2E = math.log2(math.e)
QK_SCALE_LOG2 = DA_DH ** -0.5 * LOG2E


def _cparams(*sem):
    return pltpu.CompilerParams(dimension_semantics=sem, vmem_limit_bytes=VMEM_LIMIT)


def _layernorm(x, g, b):
    mu = jnp.mean(x, axis=-1, keepdims=True)
    d = x - mu
    var = jnp.mean(d * d, axis=-1, keepdims=True)
    return d * lax.rsqrt(var + LN_EPS) * g + b


def _sigmoid(x):
    return 1.0 / (1.0 + jnp.exp(-x))


def _split2(x):
    hi = x.astype(BF16)
    lo = (x - hi.astype(F32)).astype(BF16)
    return hi, lo


def _segsum(x, jj_ref):
    hi, lo = _split2(x)
    return jnp.dot(jnp.concatenate([hi, lo], axis=1), jj_ref[...], preferred_element_type=F32)


def _seg_ones(parts):
    r = np.arange(parts * LANES)[:, None] % LANES
    c = np.arange(LANES)[None, :]
    return jnp.asarray((r // RW_HEAD) == (c // RW_HEAD), dtype=BF16)


def _diag_mask():
    r = np.arange(RW_HEAD)[:, None]
    c = np.arange(LANES)[None, :]
    return jnp.asarray((c % RW_HEAD) == r, dtype=F32)


def _mm_kernel(a_ref, w_ref, o_ref):
    o_ref[...] = jnp.dot(a_ref[...], w_ref[...], preferred_element_type=F32)


def _matmul(a, w, tm, tn, col0=0, ncols=None):
    m, k = a.shape
    n = w.shape[1] if ncols is None else ncols
    assert m % tm == 0 and n % tn == 0 and col0 % tn == 0
    j0 = col0 // tn
    return pl.pallas_call(
        _mm_kernel,
        grid=(m // tm, n // tn),
        in_specs=[pl.BlockSpec((tm, k), lambda i, j: (i, 0)),
                  pl.BlockSpec((k, tn), lambda i, j: (0, j0 + j))],
        out_specs=pl.BlockSpec((tm, tn), lambda i, j: (i, j)),
        out_shape=jax.ShapeDtypeStruct((m, n), F32),
        compiler_params=_cparams("parallel", "arbitrary"),
        name="matmul",
    )(a, w)


def _lambda(lamv_ref, lam_init):
    lv = lamv_ref[...]
    s1 = jnp.sum(lv[0:1] * lv[1:2], axis=-1, keepdims=True)
    s2 = jnp.sum(lv[2:3] * lv[3:4], axis=-1, keepdims=True)
    return jnp.exp(s1) - jnp.exp(s2) + lam_init


def _diff_combine(acc_ref, l_ref, i1, i2, lam, g, lam_init):
    o = acc_ref[i1] / l_ref[i1] - lam * (acc_ref[i2] / l_ref[i2])
    o = o * lax.rsqrt(jnp.mean(o * o, axis=-1, keepdims=True) + SUBLN_EPS) * g
    return o * (1.0 - lam_init)


def _pattn_kernel(qi_ref, ki_ref, slopes_ref, lamv_ref, g_ref, q_ref, k_ref, v_ref, o_ref,
                  qs_scr, m_scr, l_scr, acc_scr, *, lam_init, tq, rc):
    h = pl.program_id(0)
    st = pl.program_id(1)
    qb = qi_ref[st]
    kb = ki_ref[st]
    slope = slopes_ref[h]

    @pl.when(kb == 0)
    def _init():
        qs_scr[...] = (q_ref[...] * QK_SCALE_LOG2).astype(BF16)
        m_scr[...] = jnp.full(m_scr.shape, NEG_INF, F32)
        l_scr[...] = jnp.zeros(l_scr.shape, F32)
        acc_scr[...] = jnp.zeros(acc_scr.shape, F32)

    def _step(masked):
        k = k_ref[...].astype(BF16)
        v = v_ref[...].astype(BF16)
        col = lax.broadcasted_iota(jnp.int32, (1, tq), 1)
        bias = slope * ((kb - qb) * tq + col).astype(F32)
        pieces = [(c, r0) for r0 in range(0, tq, rc) for c in range(2)]

        def scores(c, r0):
            s = lax.dot_general(qs_scr[r0:r0 + rc, c * DA_DH:(c + 1) * DA_DH], k[:, c * DA_DH:(c + 1) * DA_DH],
                                _NT, preferred_element_type=F32) + bias
            if masked:
                row2 = r0 + lax.broadcasted_iota(jnp.int32, (rc, tq), 0)
                col2 = lax.broadcasted_iota(jnp.int32, (rc, tq), 1)
                s = jnp.where(col2 > row2, NEG_INF, s)
            return s

        def softmax(c, r0, s):
            m_old = m_scr[c, r0:r0 + rc]
            m_new = jnp.maximum(m_old, jnp.max(s, axis=-1, keepdims=True))
            corr = jnp.exp2(m_old - m_new)
            p = jnp.exp2(s - m_new)
            l_scr[c, r0:r0 + rc] = l_scr[c, r0:r0 + rc] * corr + jnp.sum(p, axis=-1, keepdims=True)
            m_scr[c, r0:r0 + rc] = m_new
            return p.astype(BF16), corr

        def values(c, r0, p, corr):
            acc_scr[c, r0:r0 + rc] = acc_scr[c, r0:r0 + rc] * corr + jnp.dot(p, v, preferred_element_type=F32)

        n = len(pieces)
        s_q, p_q = {}, {}
        for i in range(n + 2):
            if i < n:
                s_q[i] = scores(*pieces[i])
            if 1 <= i <= n:
                p_q[i - 1] = softmax(*pieces[i - 1], s_q.pop(i - 1))
            if i >= 2:
                values(*pieces[i - 2], *p_q.pop(i - 2))

    @pl.when(kb < qb)
    def _off_diag():
        _step(False)

    @pl.when(kb == qb)
    def _diag():
        _step(True)
        lam = _lambda(lamv_ref, lam_init)
        o_ref[...] = _diff_combine(acc_scr, l_scr, 0, 1, lam, g_ref[...], lam_init).astype(o_ref.dtype)


def _prompt_attention(q, k, v, lamv, subln_g, lam_init, tq, rc):
    t = q.shape[0]
    assert t % tq == 0 and tq % rc == 0
    nq = t // tq
    qi = np.concatenate([np.full(q + 1, q, np.int32) for q in range(nq)])
    ki = np.concatenate([np.arange(q + 1, dtype=np.int32) for q in range(nq)])
    slopes = jnp.asarray(LOG2E * 2.0 ** (-8.0 * np.arange(1, DA_HEADS + 1) / DA_HEADS), F32)
    grid_spec = pltpu.PrefetchScalarGridSpec(
        num_scalar_prefetch=3,
        grid=(DA_HEADS, len(qi)),
        in_specs=[
            pl.BlockSpec((4, DA_DH), lambda h, s, qi, ki, sl: (0, 0)),
            pl.BlockSpec((1, DA_HV), lambda h, s, qi, ki, sl: (0, 0)),
            pl.BlockSpec((tq, DA_HV), lambda h, s, qi, ki, sl: (qi[s], h)),
            pl.BlockSpec((tq, DA_HV), lambda h, s, qi, ki, sl: (ki[s], h)),
            pl.BlockSpec((tq, DA_HV), lambda h, s, qi, ki, sl: (ki[s], h)),
        ],
        out_specs=pl.BlockSpec((tq, DA_HV), lambda h, s, qi, ki, sl: (qi[s], h)),
        scratch_shapes=[
            pltpu.VMEM((tq, DA_HV), BF16),
            pltpu.VMEM((2, tq, 1), F32),
            pltpu.VMEM((2, tq, 1), F32),
            pltpu.VMEM((2, tq, DA_HV), F32),
        ],
    )
    return pl.pallas_call(
        functools.partial(_pattn_kernel, lam_init=lam_init, tq=tq, rc=rc),
        grid_spec=grid_spec,
        out_shape=jax.ShapeDtypeStruct((t, DA_WIDTH), BF16),
        compiler_params=_cparams("parallel", "arbitrary"),
        name="prompt_attention",
    )(jnp.asarray(qi), jnp.asarray(ki), slopes, lamv, subln_g, q, k, v)


def _sattn_kernel(pt_ref, lamv_ref, g_ref, q_ref, kn_ref, vn_ref, *rest, pages_per_step, lam_init):
    pp = pages_per_step
    kp_refs = rest[:pp]
    vp_refs = rest[pp:2 * pp]
    o_ref = rest[2 * pp]
    qs_scr, bias_scr, m_scr, l_scr, acc_scr = rest[2 * pp + 1:]
    st = pl.program_id(1)
    n_steps = pl.num_programs(1)
    rows = q_ref.shape[0]
    cols = PAGE_SIZE * DA_HEADS
    past_len = n_steps * pp * PAGE_SIZE

    head_r = lax.broadcasted_iota(jnp.int32, (rows, 1), 0) % DA_HEADS
    slope = LOG2E * jnp.exp2((head_r + 1).astype(F32) * (-8.0 / DA_HEADS))

    @pl.when(st == 0)
    def _init():
        qs_scr[...] = (q_ref[...] * QK_SCALE_LOG2).astype(BF16)
        lane = lax.broadcasted_iota(jnp.int32, (rows, cols), 1)
        row = lax.broadcasted_iota(jnp.int32, (rows, cols), 0)
        same_head = (lane % DA_HEADS) == (row % DA_HEADS)
        bias_scr[...] = jnp.where(same_head, slope * (lane // DA_HEADS).astype(F32), NEG_INF)
        m_scr[...] = jnp.full(m_scr.shape, NEG_INF, F32)
        l_scr[...] = jnp.zeros(l_scr.shape, F32)
        acc_scr[...] = jnp.zeros(acc_scr.shape, F32)

    def _block(k_ref, v_ref, bias):
        k = k_ref[...].astype(BF16)
        v = v_ref[...].astype(BF16)
        ps, corrs = [], []
        for c in range(2):
            s = lax.dot_general(qs_scr[:, c * DA_DH:(c + 1) * DA_DH], k[:, c * DA_DH:(c + 1) * DA_DH], _NT,
                                preferred_element_type=F32) + bias
            m_old = m_scr[c]
            m_new = jnp.maximum(m_old, jnp.max(s, axis=-1, keepdims=True))
            corr = jnp.exp2(m_old - m_new)
            p = jnp.exp2(s - m_new)
            l_scr[c] = l_scr[c] * corr + jnp.sum(p, axis=-1, keepdims=True)
            m_scr[c] = m_new
            ps.append(p.astype(BF16))
            corrs.append(corr)
        pv = jnp.dot(jnp.concatenate(ps, axis=0), v, preferred_element_type=F32)
        for c in range(2):
            acc_scr[c] = acc_scr[c] * corrs[c] + pv[c * rows:(c + 1) * rows]

    for p in range(pp):
        offset = ((st * pp + p) * PAGE_SIZE - past_len).astype(F32)
        _block(kp_refs[p], vp_refs[p], bias_scr[...] + slope * offset)

    @pl.when(st == n_steps - 1)
    def _last():
        lane = lax.broadcasted_iota(jnp.int32, (rows, cols), 1)
        row = lax.broadcasted_iota(jnp.int32, (rows, cols), 0)
        causal = (lane // DA_HEADS) <= (row // DA_HEADS)
        _block(kn_ref, vn_ref, jnp.where(causal, bias_scr[...], NEG_INF))
        lam = _lambda(lamv_ref, lam_init)
        o_ref[...] = _diff_combine(acc_scr, l_scr, 0, 1, lam, g_ref[...], lam_init).astype(o_ref.dtype)


def _sample_attention(q2d, k2d, v2d, cache_k, cache_v, page_table, lamv, subln_g, lam_init, n_tok,
                      pages_per_step):
    b, n_pages = page_table.shape
    pp = pages_per_step
    assert n_pages % pp == 0
    n_phys = cache_k.shape[0]
    rows = n_tok * DA_HEADS
    cols = PAGE_SIZE * DA_HEADS
    ck = cache_k.reshape(n_phys, cols, DA_HV)
    cv = cache_v.reshape(n_phys, cols, DA_HV)
    q = q2d.reshape(b, rows, DA_HV)
    pad = ((0, 0), (0, PAGE_SIZE - n_tok), (0, 0))
    k_new = jnp.pad(k2d.reshape(b, n_tok, DA_WIDTH), pad).reshape(b, cols, DA_HV)
    v_new = jnp.pad(v2d.reshape(b, n_tok, DA_WIDTH), pad).reshape(b, cols, DA_HV)

    def page_spec(p):
        return pl.BlockSpec((None, cols, DA_HV), lambda bi, s, pt, p=p: (pt[bi * n_pages + s * pp + p], 0, 0))

    grid_spec = pltpu.PrefetchScalarGridSpec(
        num_scalar_prefetch=1,
        grid=(b, n_pages // pp),
        in_specs=[
            pl.BlockSpec((4, DA_DH), lambda bi, s, pt: (0, 0)),
            pl.BlockSpec((1, DA_HV), lambda bi, s, pt: (0, 0)),
            pl.BlockSpec((None, rows, DA_HV), lambda bi, s, pt: (bi, 0, 0)),
            pl.BlockSpec((None, cols, DA_HV), lambda bi, s, pt: (bi, 0, 0)),
            pl.BlockSpec((None, cols, DA_HV), lambda bi, s, pt: (bi, 0, 0)),
        ] + [page_spec(p) for p in range(pp)] + [page_spec(p) for p in range(pp)],
        out_specs=pl.BlockSpec((None, rows, DA_HV), lambda bi, s, pt: (bi, 0, 0)),
        scratch_shapes=[
            pltpu.VMEM((rows, DA_HV), BF16),
            pltpu.VMEM((rows, cols), F32),
            pltpu.VMEM((2, rows, 1), F32),
            pltpu.VMEM((2, rows, 1), F32),
            pltpu.VMEM((2, rows, DA_HV), F32),
        ],
    )
    out = pl.pallas_call(
        functools.partial(_sattn_kernel, pages_per_step=pp, lam_init=lam_init),
        grid_spec=grid_spec,
        out_shape=jax.ShapeDtypeStruct((b, rows, DA_HV), F32),
        compiler_params=_cparams("parallel", "arbitrary"),
        name="sample_attention",
    )(page_table.reshape(-1), lamv, subln_g, q, k_new, v_new, *([ck] * pp), *([cv] * pp))
    return out.reshape(b * n_tok, DA_WIDTH)


def _rwkv_prep_kernel(fr_ref, fk_ref, fv_ref, fl_ref, pr_ref, pk_ref, pv_ref, pl_ref,
                      mur_ref, muk_ref, muv_ref, mul_ref, w0_ref, a0_ref, kkw_ref, kaw_ref,
                      w2_ref, a2_ref, g2_ref, jj_ref,
                      r_out, w_out, k_out, v_out, kk_out, b_out, g_out,
                      cr_scr, ck_scr, cv_scr, cl_scr):
    tb = pl.program_id(1)
    tm = fr_ref.shape[0]

    @pl.when(tb == 0)
    def _load_state():
        cr_scr[...] = pr_ref[...]
        ck_scr[...] = pk_ref[...]
        cv_scr[...] = pv_ref[...]
        cl_scr[...] = pl_ref[...]

    def mixed(f_ref, carry_scr, mu_ref):
        x = f_ref[...]
        row = lax.broadcasted_iota(jnp.int32, x.shape, 0)
        shifted = jnp.where(row == 0, carry_scr[...], pltpu.roll(x, 1, 0))
        carry_scr[...] = x[tm - 1:tm, :]
        return x + (shifted - x) * mu_ref[...]

    xr = mixed(fr_ref, cr_scr, mur_ref)
    xk = mixed(fk_ref, ck_scr, muk_ref)
    xv = mixed(fv_ref, cv_scr, muv_ref)
    xl = mixed(fl_ref, cl_scr, mul_ref)

    dw = jnp.tanh(xl[:, 0:LORA_PAD]).astype(BF16)
    da = xl[:, LORA_PAD:2 * LORA_PAD].astype(BF16)
    dg = _sigmoid(xl[:, 2 * LORA_PAD:]).astype(BF16)
    wl = w0_ref[...] + jnp.dot(dw, w2_ref[...], preferred_element_type=F32)
    neg = -wl
    softplus = jnp.maximum(neg, 0.0) + jnp.log(1.0 + jnp.exp(-jnp.abs(neg)))
    log_w = -jnp.exp(-softplus - 0.5)
    a = _sigmoid(a0_ref[...] + jnp.dot(da, a2_ref[...], preferred_element_type=F32))
    g = jnp.dot(dg, g2_ref[...], preferred_element_type=F32)

    kk0 = xk * kkw_ref[...]
    sq = kk0 * kk0
    ss = jnp.concatenate([_segsum(sq[:, n * LANES:(n + 1) * LANES], jj_ref)
                          for n in range(RW_WIDTH // LANES)], axis=1)
    kk = kk0 / jnp.maximum(jnp.sqrt(ss), 1e-12)

    r_out[...] = xr
    w_out[...] = log_w
    k_out[...] = xk * (1.0 + (a - 1.0) * kaw_ref[...])
    v_out[...] = xv
    kk_out[...] = kk
    b_out[...] = kk * a
    g_out[...] = g


def _rwkv_prep(proj, lora, prev_pad, mu_pad, w0, a0, k_k, k_a, w2p, a2p, g2b, n_batch, tm):
    m = proj.shape[0]
    t = m // n_batch
    assert t % tm == 0
    nt = t // tm
    base = 0
    prev3 = prev_pad.reshape(n_batch, 1, FEAT_PAD)
    mu2 = mu_pad.reshape(1, FEAT_PAD)

    def feat(sec):
        return pl.BlockSpec((tm, RW_WIDTH), lambda b, i, sec=sec: (b * nt + i, base + sec))

    def prev(sec):
        return pl.BlockSpec((None, 1, RW_WIDTH), lambda b, i, sec=sec: (b, 0, sec))

    def mu(sec):
        return pl.BlockSpec((1, RW_WIDTH), lambda b, i, sec=sec: (0, sec))

    vec = pl.BlockSpec((1, RW_WIDTH), lambda b, i: (0, 0))
    out = pl.BlockSpec((tm, RW_WIDTH), lambda b, i: (b * nt + i, 0))
    in_specs = [
        feat(0), feat(1), feat(2),
        pl.BlockSpec((tm, LORA_BLOCK), lambda b, i: (b * nt + i, 0)),
        prev(0), prev(1), prev(2),
        pl.BlockSpec((None, 1, LORA_BLOCK), lambda b, i: (b, 0, 3 * RW_WIDTH // LORA_BLOCK)),
        mu(0), mu(1), mu(2),
        pl.BlockSpec((1, LORA_BLOCK), lambda b, i: (0, 3 * RW_WIDTH // LORA_BLOCK)),
        vec, vec, vec, vec,
        pl.BlockSpec((LORA_PAD, RW_WIDTH), lambda b, i: (0, 0)),
        pl.BlockSpec((LORA_PAD, RW_WIDTH), lambda b, i: (0, 0)),
        pl.BlockSpec((GATE_LORA, RW_WIDTH), lambda b, i: (0, 0)),
        pl.BlockSpec((2 * LANES, LANES), lambda b, i: (0, 0)),
    ]
    shp = jax.ShapeDtypeStruct((m, RW_WIDTH), F32)
    return pl.pallas_call(
        _rwkv_prep_kernel,
        grid=(n_batch, nt),
        in_specs=in_specs,
        out_specs=[out] * 7,
        out_shape=[shp] * 7,
        scratch_shapes=[pltpu.VMEM((1, RW_WIDTH), F32)] * 3 + [pltpu.VMEM((1, LORA_BLOCK), F32)],
        compiler_params=_cparams("parallel", "arbitrary"),
        name="rwkv_prep",
    )(proj, proj, proj, lora, prev3, prev3, prev3, prev3, mu2, mu2, mu2, mu2,
      w0.reshape(1, -1), a0.reshape(1, -1), k_k.reshape(1, -1), k_a.reshape(1, -1),
      w2p, a2p, g2b, _seg_ones(2))


N_TILES = RW_WIDTH // LANES
S_ROWS = N_TILES * RW_HEAD


def _wkv_kernel(r_ref, w_ref, k_ref, v_ref, kk_ref, b_ref, g_ref, s0_ref,
                lng_ref, lnb_ref, rk_ref, jj2_ref, jj3_ref, dm_ref,
                o_ref, st_ref, s_scr, o_scr):
    tb = pl.program_id(1)
    n_tb = pl.num_programs(1)
    t_blk = r_ref.shape[0]

    @pl.when(tb == 0)
    def _load_state():
        s_scr[...] = s0_ref[...]

    dm = dm_ref[...]
    dm_all = jnp.concatenate([dm] * N_TILES, axis=0)
    dm_all_b = dm_all.astype(BF16)

    def expand(row):
        return jnp.concatenate(
            [jnp.broadcast_to(row[:, n * LANES:(n + 1) * LANES], (RW_HEAD, LANES)) for n in range(N_TILES)],
            axis=0)

    def step(t, carry):
        s = s_scr[...]
        kkx = expand(kk_ref[pl.ds(t, 1), :])
        wx = expand(jnp.exp(w_ref[pl.ds(t, 1), :]))
        bx = expand(b_ref[pl.ds(t, 1), :])
        kx = expand(k_ref[pl.ds(t, 1), :])
        rx = expand(r_ref[pl.ds(t, 1), :])
        sa = -_segsum(s * kkx, jj2_ref)
        v_row = v_ref[pl.ds(t, 1), :]
        v_hi = v_row.astype(BF16)
        rem = v_row - v_hi.astype(F32)
        v_mid = rem.astype(BF16)
        v_lo = (rem - v_mid.astype(F32)).astype(BF16)
        v_parts = jnp.concatenate([expand(p) * dm_all_b for p in (v_hi, v_mid, v_lo)], axis=1)
        vx = jnp.dot(v_parts, jj3_ref[...], preferred_element_type=F32)
        s_new = s * wx + sa * bx + vx * kx
        s_scr[...] = s_new
        ox = _segsum(s_new * rx, jj2_ref) * dm_all
        o_row = jnp.concatenate(
            [jnp.sum(ox[n * RW_HEAD:(n + 1) * RW_HEAD], axis=0, keepdims=True) for n in range(N_TILES)], axis=1)
        o_scr[pl.ds(t, 1), :] = o_row
        return carry

    lax.fori_loop(0, t_blk, step, 0)

    for n in range(N_TILES):
        sl = slice(n * LANES, (n + 1) * LANES)
        o = o_scr[:, sl]
        mu = _segsum(o, jj2_ref) * (1.0 / RW_HEAD)
        d = o - mu
        var = _segsum(d * d, jj2_ref) * (1.0 / RW_HEAD)
        on = d * lax.rsqrt(var + RW_GN_EPS) * lng_ref[:, sl] + lnb_ref[:, sl]
        bonus = _segsum(r_ref[:, sl] * k_ref[:, sl] * rk_ref[:, sl], jj2_ref) * v_ref[:, sl]
        o_ref[:, sl] = ((on + bonus) * g_ref[:, sl]).astype(o_ref.dtype)

    @pl.when(tb == n_tb - 1)
    def _store_state():
        st_ref[...] = s_scr[...]


def _state_to_tiles(s):
    b = s.shape[0]
    return s.reshape(b, N_TILES, 2, RW_HEAD, RW_HEAD).transpose(0, 1, 3, 2, 4).reshape(b, S_ROWS, LANES)


def _tiles_to_state(s):
    b = s.shape[0]
    return s.reshape(b, N_TILES, RW_HEAD, 2, RW_HEAD).transpose(0, 1, 3, 2, 4).reshape(
        b, RW_HEADS, RW_HEAD, RW_HEAD)


def _wkv(feats, s0_tiles, lnx_g, lnx_b, r_k, n_batch, t_blk):
    r, w, k, v, kk, bb, g = feats
    m = r.shape[0]
    t = m // n_batch
    assert t % t_blk == 0
    nt = t // t_blk
    blk = pl.BlockSpec((t_blk, RW_WIDTH), lambda b, i: (b * nt + i, 0))
    vec = pl.BlockSpec((1, RW_WIDTH), lambda b, i: (0, 0))
    st = pl.BlockSpec((None, S_ROWS, LANES), lambda b, i: (b, 0, 0))
    return pl.pallas_call(
        _wkv_kernel,
        grid=(n_batch, nt),
        in_specs=[blk] * 7 + [st, vec, vec, vec,
                              pl.BlockSpec((2 * LANES, LANES), lambda b, i: (0, 0)),
                              pl.BlockSpec((3 * LANES, LANES), lambda b, i: (0, 0)),
                              pl.BlockSpec((RW_HEAD, LANES), lambda b, i: (0, 0))],
        out_specs=[blk, st],
        out_shape=[jax.ShapeDtypeStruct((m, RW_WIDTH), BF16),
                   jax.ShapeDtypeStruct((n_batch, S_ROWS, LANES), F32)],
        scratch_shapes=[pltpu.VMEM((S_ROWS, LANES), F32), pltpu.VMEM((t_blk, RW_WIDTH), F32)],
        compiler_params=_cparams("parallel", "arbitrary"),
        name="wkv",
    )(r, w, k, v, kk, bb, g, s0_tiles, lnx_g.reshape(1, -1), lnx_b.reshape(1, -1), r_k.reshape(1, -1),
      _seg_ones(2), _seg_ones(3), _diag_mask())


_NN = (((1,), (0,)), ((), ()))


def _mm3(a, b, dims=_NN):
    zero = jnp.zeros_like(b[0])
    lhs = jnp.concatenate([a[0], a[1]], axis=1)
    if dims == _NN:
        rhs = jnp.concatenate([jnp.concatenate([b[0], b[1]], axis=1), jnp.concatenate([b[0], zero], axis=1)], axis=0)
    else:
        rhs = jnp.concatenate([jnp.concatenate([b[0], b[0]], axis=1), jnp.concatenate([b[1], zero], axis=1)], axis=0)
    out = lax.dot_general(lhs, rhs, dims, preferred_element_type=F32)
    n = out.shape[1] // 2
    return out[:, :n] + out[:, n:]


def _wkv_chunk_tiles(lw, r, k, v, kk, b, ys_in, c_len):
    rows = 2 * c_len
    lane = lax.broadcasted_iota(jnp.int32, (1, LANES), 1)
    head_a = (lane < RW_HEAD).astype(F32)
    head_b = 1.0 - head_a

    def stack(x):
        return jnp.concatenate([x * head_a, x * head_b], axis=0)

    def each(fn, *lists):
        return [fn(*args) for args in zip(*lists)]

    tr = lax.broadcasted_iota(jnp.int32, (c_len, c_len), 0)
    tc = lax.broadcasted_iota(jnp.int32, (c_len, c_len), 1)
    tril = (tc <= tr).astype(BF16)

    def cumsum(x):
        l1 = x.astype(BF16)
        rem = x - l1.astype(F32)
        l2 = rem.astype(BF16)
        l3 = (rem - l2.astype(F32)).astype(BF16)
        return (jnp.dot(tril, l1, preferred_element_type=F32) + jnp.dot(tril, l2, preferred_element_type=F32)
                + jnp.dot(tril, l3, preferred_element_type=F32))

    cs = each(cumsum, lw)
    cs_last = [c[c_len - 1:c_len, :] for c in cs]
    g_rem = each(lambda c, cl: jnp.exp(cl - c), cs, cs_last)
    g_inv = each(lambda c: jnp.exp(-c), cs)
    a2f = each(lambda x, c, l: stack(-x * jnp.exp(c - l)), kk, cs, lw)
    r2f = each(lambda x, c: stack(x * jnp.exp(c)), r, cs)
    v2f = each(stack, v)
    a2 = each(_split2, a2f)
    r2 = each(_split2, r2f)
    v2 = each(_split2, v2f)
    b2 = each(lambda x, g: _split2(stack(x * g)), b, g_inv)
    k2 = each(lambda x, g: _split2(stack(x * g)), k, g_inv)
    bg2 = each(lambda x, g: _split2(stack(x * g)), b, g_rem)
    kg2 = each(lambda x, g: _split2(stack(x * g)), k, g_rem)

    rr = lax.broadcasted_iota(jnp.int32, (rows, rows), 0)
    cc = lax.broadcasted_iota(jnp.int32, (rows, rows), 1)
    same = (rr // c_len) == (cc // c_len)
    strict = same & ((cc % c_len) < (rr % c_len))
    lower = same & ((cc % c_len) <= (rr % c_len))
    eye = jnp.where(rr == cc, 1.0, 0.0)

    ar2 = each(lambda x, y: (jnp.concatenate([x[0], y[0]], axis=0), jnp.concatenate([x[1], y[1]], axis=0)), a2, r2)
    gb = each(lambda x, y: _mm3(x, y, _NT), ar2, b2)
    gk = each(lambda x, y: _mm3(x, y, _NT), ar2, k2)
    l_mat = [jnp.where(strict, x[:rows], 0.0) for x in gb]
    ak = [jnp.where(strict, x[:rows], 0.0) for x in gk]
    rb = [jnp.where(lower, x[rows:], 0.0) for x in gb]
    rk = [jnp.where(lower, x[rows:], 0.0) for x in gk]

    t_mat = [eye + x for x in l_mat]
    p_mat = l_mat
    for _ in range(max(int(math.log2(c_len)) - 1, 0)):
        ps = each(_split2, p_mat)
        p_mat = each(lambda x: _mm3(x, x), ps)
        t_mat = each(lambda t, p: t + _mm3(_split2(t), _split2(p)), t_mat, p_mat)

    akv = each(lambda x, y: _mm3(_split2(x), y), ak, v2)
    wu = each(lambda t, x, y: _mm3(_split2(t), _split2(jnp.concatenate([x, y], axis=1))),
              t_mat, a2f, akv)
    one = lambda x, y, dims=_NN: lax.dot_general(x.astype(BF16), y.astype(BF16), dims, preferred_element_type=F32)
    qo = each(one, rb, wu)
    q2 = each(lambda x, y: x + y[:, :LANES], r2f, qo)
    o0 = each(lambda q, x, y: q[:, LANES:] + one(x, y), qo, rk, v2f)
    mn = each(lambda x, y: _mm3(_split2(x.T), y), wu, bg2)
    n_mat = each(lambda m, x, y: m[LANES:] + _mm3(_split2(x.T), y), mn, v2f, kg2)

    ys = each(_split2, ys_in)
    o2 = each(lambda q, y, o: one(q, y, _NT) + o, q2, ys_in, o0)
    o = [x[:c_len] + x[c_len:] for x in o2]
    y_new = each(lambda y, cl, ysp, m, nm: y * jnp.exp(cl) + _mm3(ysp, _split2(m[:LANES])) + nm,
                 ys_in, cs_last, ys, mn, n_mat)
    return o, y_new


def _wkv_chunk_kernel(lw_ref, r_ref, k_ref, v_ref, kk_ref, b_ref, g_ref, y0_ref,
                      lng_ref, lnb_ref, rk_ref, jj2_ref, o_ref, yt_ref, y_scr):
    c = pl.program_id(2)
    n_c = pl.num_programs(2)
    c_len = r_ref.shape[0]
    tiles = r_ref.shape[1] // LANES

    @pl.when(c == 0)
    def _load_state():
        y_scr[...] = y0_ref[...]

    sls = [slice(t * LANES, (t + 1) * LANES) for t in range(tiles)]
    split = lambda ref: [ref[:, sl] for sl in sls]
    r, k, v = split(r_ref), split(k_ref), split(v_ref)
    o, y_new = _wkv_chunk_tiles(split(lw_ref), r, k, v, split(kk_ref), split(b_ref),
                                [y_scr[t] for t in range(tiles)], c_len)
    for t, sl in enumerate(sls):
        y_scr[t] = y_new[t]
        mu = _segsum(o[t], jj2_ref) * (1.0 / RW_HEAD)
        d = o[t] - mu
        var = _segsum(d * d, jj2_ref) * (1.0 / RW_HEAD)
        on = d * lax.rsqrt(var + RW_GN_EPS) * lng_ref[:, sl] + lnb_ref[:, sl]
        bonus = _segsum(r[t] * k[t] * rk_ref[:, sl], jj2_ref) * v[t]
        o_ref[:, sl] = ((on + bonus) * g_ref[:, sl]).astype(o_ref.dtype)

    @pl.when(c == n_c - 1)
    def _store_state():
        yt_ref[...] = y_scr[...]


def _state_to_blockdiag(s):
    b = s.shape[0]
    s5 = s.reshape(b, N_TILES, 2, RW_HEAD, RW_HEAD)
    z = jnp.zeros((b, N_TILES, RW_HEAD, RW_HEAD), s.dtype)
    top = jnp.concatenate([s5[:, :, 0], z], axis=-1)
    bot = jnp.concatenate([z, s5[:, :, 1]], axis=-1)
    return jnp.concatenate([top, bot], axis=-2)


def _blockdiag_to_state(y):
    b = y.shape[0]
    s = jnp.stack([y[:, :, :RW_HEAD, :RW_HEAD], y[:, :, RW_HEAD:, RW_HEAD:]], axis=2)
    return s.reshape(b, RW_HEADS, RW_HEAD, RW_HEAD)


def _wkv_chunked(feats, y0, lnx_g, lnx_b, r_k, n_batch, c_len, tiles):
    r, lw, k, v, kk, bb, g = feats
    m = r.shape[0]
    t = m // n_batch
    assert t % c_len == 0 and N_TILES % tiles == 0
    nc = t // c_len
    wid = tiles * LANES
    blk = pl.BlockSpec((c_len, wid), lambda b, gi, c: (b * nc + c, gi))
    vec = pl.BlockSpec((1, wid), lambda b, gi, c: (0, gi))
    st = pl.BlockSpec((None, tiles, LANES, LANES), lambda b, gi, c: (b, gi, 0, 0))
    return pl.pallas_call(
        _wkv_chunk_kernel,
        grid=(n_batch, N_TILES // tiles, nc),
        in_specs=[blk] * 7 + [st, vec, vec, vec, pl.BlockSpec((2 * LANES, LANES), lambda b, gi, c: (0, 0))],
        out_specs=[blk, st],
        out_shape=[jax.ShapeDtypeStruct((m, RW_WIDTH), BF16),
                   jax.ShapeDtypeStruct((n_batch, N_TILES, LANES, LANES), F32)],
        scratch_shapes=[pltpu.VMEM((tiles, LANES, LANES), F32)],
        compiler_params=_cparams("parallel", "parallel", "arbitrary"),
        name="wkv_chunked",
    )(lw, r, k, v, kk, bb, g, y0, lnx_g.reshape(1, -1), lnx_b.reshape(1, -1), r_k.reshape(1, -1), _seg_ones(2))


def _outproj_ln_kernel(a1_ref, a2_ref, w_ref, res_ref, g_ref, b_ref, o_ref, acc_scr):
    j = pl.program_id(1)
    nn = pl.num_programs(1)
    k1 = a1_ref.shape[1]
    mix = (jnp.dot(a1_ref[...], w_ref[:k1, :], preferred_element_type=F32)
           + jnp.dot(a2_ref[...], w_ref[k1:, :], preferred_element_type=F32))
    acc_scr[j] = ALPHA * res_ref[...] + mix

    @pl.when(j == nn - 1)
    def _finish():
        y = jnp.concatenate([acc_scr[n] for n in range(acc_scr.shape[0])], axis=1)
        o_ref[...] = _layernorm(y, g_ref[...], b_ref[...])


def _outproj_ln(a1, a2, w, res, g, b, tm, tn):
    m, k1 = a1.shape
    k2 = a2.shape[1]
    n = w.shape[1]
    assert m % tm == 0 and n % tn == 0
    return pl.pallas_call(
        _outproj_ln_kernel,
        grid=(m // tm, n // tn),
        in_specs=[pl.BlockSpec((tm, k1), lambda i, j: (i, 0)),
                  pl.BlockSpec((tm, k2), lambda i, j: (i, 0)),
                  pl.BlockSpec((k1 + k2, tn), lambda i, j: (0, j)),
                  pl.BlockSpec((tm, tn), lambda i, j: (i, j)),
                  pl.BlockSpec((1, n), lambda i, j: (0, 0)),
                  pl.BlockSpec((1, n), lambda i, j: (0, 0))],
        out_specs=pl.BlockSpec((tm, n), lambda i, j: (i, 0)),
        out_shape=jax.ShapeDtypeStruct((m, n), F32),
        scratch_shapes=[pltpu.VMEM((n // tn, tm, tn), F32)],
        compiler_params=_cparams("parallel", "arbitrary"),
        name="outproj_ln1",
    )(a1, a2, w, res, g.reshape(1, -1), b.reshape(1, -1))


def _memattn_kernel(x_ref, mk_ref, mv_ref, wq_ref, wo_ref, g_ref, b_ref, o_ref):
    x = x_ref[...]
    q = jnp.dot(x.astype(BF16), wq_ref[...], preferred_element_type=F32) * (MEM_DH ** -0.5)
    mk = mk_ref[...].astype(BF16)
    mv = mv_ref[...].astype(BF16)
    heads = []
    for h in range(MEM_HEADS):
        sl = slice(h * MEM_DH, (h + 1) * MEM_DH)
        s = lax.dot_general(q[:, sl].astype(BF16), mk[:, sl], _NT, preferred_element_type=F32)
        p = jnp.exp(s - jnp.max(s, axis=-1, keepdims=True))
        l = jnp.sum(p, axis=-1, keepdims=True)
        heads.append(jnp.dot(p.astype(BF16), mv[:, sl], preferred_element_type=F32) / l)
    o = jnp.concatenate(heads, axis=1).astype(BF16)
    y = ALPHA * x + jnp.dot(o, wo_ref[...], preferred_element_type=F32)
    o_ref[...] = _layernorm(y, g_ref[...], b_ref[...])


def _memattn_ln(x, mk, mv, wq, wo, g, b, n_batch, tm):
    m, d = x.shape
    t = m // n_batch
    assert t % tm == 0
    nt = t // tm
    return pl.pallas_call(
        _memattn_kernel,
        grid=(n_batch, nt),
        in_specs=[pl.BlockSpec((tm, d), lambda bi, i: (bi * nt + i, 0)),
                  pl.BlockSpec((None, MEM_LEN, MEM_WIDTH), lambda bi, i: (bi, 0, 0)),
                  pl.BlockSpec((None, MEM_LEN, MEM_WIDTH), lambda bi, i: (bi, 0, 0)),
                  pl.BlockSpec((d, MEM_WIDTH), lambda bi, i: (0, 0)),
                  pl.BlockSpec((MEM_WIDTH, d), lambda bi, i: (0, 0)),
                  pl.BlockSpec((1, d), lambda bi, i: (0, 0)),
                  pl.BlockSpec((1, d), lambda bi, i: (0, 0))],
        out_specs=pl.BlockSpec((tm, d), lambda bi, i: (bi * nt + i, 0)),
        out_shape=jax.ShapeDtypeStruct((m, d), F32),
        compiler_params=_cparams("parallel", "arbitrary"),
        name="memattn_ln2",
    )(x, mk, mv, wq, wo, g.reshape(1, -1), b.reshape(1, -1))


def _swiglu_kernel(x_ref, wg_ref, wu_ref, wd_ref, g_ref, b_ref, o_ref, xb_scr):
    f = pl.program_id(1)
    nf = pl.num_programs(1)

    @pl.when(f == 0)
    def _init():
        x = x_ref[...]
        xb_scr[...] = x.astype(BF16)
        o_ref[...] = ALPHA * x

    xb = xb_scr[...]
    gate = jnp.dot(xb, wg_ref[...], preferred_element_type=F32)
    up = jnp.dot(xb, wu_ref[...], preferred_element_type=F32)
    hidden = (gate * _sigmoid(gate) * up).astype(BF16)
    o_ref[...] += jnp.dot(hidden, wd_ref[...], preferred_element_type=F32)

    @pl.when(f == nf - 1)
    def _finish():
        o_ref[...] = _layernorm(o_ref[...], g_ref[...], b_ref[...])


def _swiglu_ln(x, wg, wu, wd, g, b, tm, tf):
    m, d = x.shape
    f = wg.shape[1]
    assert m % tm == 0 and f % tf == 0
    return pl.pallas_call(
        _swiglu_kernel,
        grid=(m // tm, f // tf),
        in_specs=[pl.BlockSpec((tm, d), lambda i, j: (i, 0), pipeline_mode=pl.Buffered(1)),
                  pl.BlockSpec((d, tf), lambda i, j: (0, j)),
                  pl.BlockSpec((d, tf), lambda i, j: (0, j)),
                  pl.BlockSpec((tf, d), lambda i, j: (j, 0)),
                  pl.BlockSpec((1, d), lambda i, j: (0, 0)),
                  pl.BlockSpec((1, d), lambda i, j: (0, 0))],
        out_specs=pl.BlockSpec((tm, d), lambda i, j: (i, 0), pipeline_mode=pl.Buffered(1)),
        out_shape=jax.ShapeDtypeStruct((m, d), F32),
        scratch_shapes=[pltpu.VMEM((tm, d), BF16)],
        compiler_params=_cparams("parallel", "arbitrary"),
        name="swiglu_ln3",
    )(x, wg, wu, wd, g.reshape(1, -1), b.reshape(1, -1))


def _pad_feat(x):
    c = 3 * RW_WIDTH
    z = jnp.zeros(x.shape[:-1] + (LORA_PAD - DECAY_LORA,), x.dtype)
    return jnp.concatenate([x[..., :c + DECAY_LORA], z, x[..., c + DECAY_LORA:c + DECAY_LORA + AAA_LORA], z,
                            x[..., c + DECAY_LORA + AAA_LORA:]], axis=-1)


def _unpad_feat(x):
    c = 3 * RW_WIDTH
    return jnp.concatenate([x[..., :c + DECAY_LORA], x[..., c + LORA_PAD:c + LORA_PAD + AAA_LORA],
                            x[..., c + 2 * LORA_PAD:]], axis=-1)


def _pad_rows(w, rows):
    return jnp.concatenate([w, jnp.zeros((rows - w.shape[0],) + w.shape[1:], w.dtype)], axis=0)


def _layer(x2d, n_batch, attn_fn, wkv0, shift0_pad, mem_k, mem_v, p, cfg):
    xb = x2d.astype(BF16)
    mm = lambda col0, ncols: _matmul(xb, p["w_in"], cfg["mm_tm"], cfg["mm_tn"], col0, ncols)
    q = mm(0, DA_WIDTH)
    k = mm(DA_WIDTH, DA_WIDTH)
    v = mm(2 * DA_WIDTH, DA_WIDTH)
    feat = mm(3 * DA_WIDTH, 3 * RW_WIDTH)
    lora = _matmul(xb, p["w_in_lora"], cfg["mm_tm"], LORA_BLOCK)
    t = x2d.shape[0] // n_batch
    last = jnp.concatenate([feat.reshape(n_batch, t, -1)[:, t - 1], lora.reshape(n_batch, t, -1)[:, t - 1]], -1)
    o_a = attn_fn(q, k, v)
    feats = _rwkv_prep(feat, lora, shift0_pad, p["mu_shift"], p["w0"], p["a0"], p["k_k"], p["k_a"],
                       p["w2"], p["a2"], p["g2"], n_batch, cfg["prep_tm"])
    if cfg["wkv_chunk"]:
        o_r, y_t = _wkv_chunked(feats, _state_to_blockdiag(wkv0), p["lnx_g"], p["lnx_b"], p["r_k"], n_batch,
                                cfg["wkv_chunk"], cfg["wkv_tiles"])
        wkv_t = _blockdiag_to_state(y_t)
    else:
        o_r, s_t = _wkv(feats, _state_to_tiles(wkv0), p["lnx_g"], p["lnx_b"], p["r_k"], n_batch, cfg["wkv_tb"])
        wkv_t = _tiles_to_state(s_t)
    x1 = _outproj_ln(o_a, o_r, p["w_out"], x2d, p["ln1_g"], p["ln1_b"], cfg["op_tm"], cfg["op_tn"])
    x2 = _memattn_ln(x1, mem_k, mem_v, p["wq_m"], p["wo_m"], p["ln2_g"], p["ln2_b"], n_batch, cfg["ma_tm"])
    y = _swiglu_ln(x2, p["w_gate"], p["w_up"], p["w_down"], p["ln3_g"], p["ln3_b"], cfg["ff_tm"], cfg["ff_tf"])
    return y, k, v, last, wkv_t


_PROMPT_CFG = dict(mm_tm=1024, mm_tn=512, prep_tm=256, wkv_chunk=64, wkv_tiles=8, op_tm=512, op_tn=512,
                   ma_tm=256, ff_tm=512, ff_tf=256, attn_tq=1024, attn_rc=512)
_SAMPLE_CFG = dict(mm_tm=256, mm_tn=512, prep_tm=8, wkv_chunk=0, wkv_tb=8, op_tm=256, op_tn=512, ma_tm=8,
                   ff_tm=256, ff_tf=256, pages_per_step=8)


def kernel(x_prompt, x_sample, cache_k, cache_v, cache_mem_k, cache_mem_v, state_wkv, state_shift, page_table, mem_prompt, w_in, lam_q1, lam_k1, lam_q2, lam_k2, subln_g, mu_shift, w0, w2, a0, a2, g2, k_k, k_a, r_k, lnx_g, lnx_b, w_out, ln1_g, ln1_b, wq_m, wk_m, wv_m, wo_m, ln2_g, ln2_b, w_gate, w_up, w_down, ln3_g, ln3_b):
    bp, tp, d = x_prompt.shape
    bs, ts, _ = x_sample.shape
    yp = x_prompt.reshape(bp * tp, d)
    ys = x_sample.reshape(bs * ts, d)
    outs = [[] for _ in range(10)]
    for l in range(DEPTH):
        lam_init = 0.8 - 0.6 * math.exp(-0.3 * l)
        c = 3 * DA_WIDTH + 3 * RW_WIDTH
        w_in_b = w_in[l].astype(BF16)
        zc = jnp.zeros((d, LORA_PAD - DECAY_LORA), BF16)
        w_lora = w_in_b[:, c:]
        w_in_lora = jnp.concatenate(
            [w_lora[:, :DECAY_LORA], zc, w_lora[:, DECAY_LORA:DECAY_LORA + AAA_LORA], zc,
             w_lora[:, DECAY_LORA + AAA_LORA:]], axis=1)
        p = dict(
            w_in=w_in_b, w_in_lora=w_in_lora, mu_shift=_pad_feat(mu_shift[l]), w0=w0[l], a0=a0[l], k_k=k_k[l], k_a=k_a[l],
            w2=_pad_rows(w2[l], LORA_PAD).astype(BF16), a2=_pad_rows(a2[l], LORA_PAD).astype(BF16),
            g2=g2[l].astype(BF16), r_k=r_k[l], lnx_g=lnx_g[l], lnx_b=lnx_b[l],
            w_out=w_out[l].astype(BF16), ln1_g=ln1_g[l], ln1_b=ln1_b[l],
            wq_m=wq_m[l].astype(BF16), wo_m=wo_m[l].astype(BF16), ln2_g=ln2_g[l], ln2_b=ln2_b[l],
            w_gate=w_gate[l].astype(BF16), w_up=w_up[l].astype(BF16), w_down=w_down[l].astype(BF16),
            ln3_g=ln3_g[l], ln3_b=ln3_b[l])
        lamv = jnp.stack([lam_q1[l], lam_k1[l], lam_q2[l], lam_k2[l]])
        g_sub = subln_g[l].reshape(1, DA_HV)

        w_kv = jnp.concatenate([wk_m[l], wv_m[l]], axis=1).astype(BF16)
        mem_kv = _matmul(mem_prompt.reshape(bp * MEM_LEN, d).astype(BF16), w_kv, MEM_LEN, MEM_WIDTH)
        mk_p = mem_kv[:, :MEM_WIDTH].reshape(bp, MEM_LEN, MEM_WIDTH)
        mv_p = mem_kv[:, MEM_WIDTH:].reshape(bp, MEM_LEN, MEM_WIDTH)
        attn_p = functools.partial(_prompt_attention, lamv=lamv, subln_g=g_sub, lam_init=lam_init,
                                   tq=_PROMPT_CFG["attn_tq"], rc=_PROMPT_CFG["attn_rc"])
        yp, k_p, v_p, last_p, wkv_p = _layer(yp, bp, attn_p, jnp.zeros((bp, RW_HEADS, RW_HEAD, RW_HEAD), F32),
                                   jnp.zeros((bp, FEAT_PAD), F32), mk_p, mv_p, p, _PROMPT_CFG)

        def attn_s(q, k, v, l=l, lamv=lamv, g_sub=g_sub, lam_init=lam_init):
            o = _sample_attention(q, k, v, cache_k[l], cache_v[l], page_table, lamv, g_sub, lam_init, ts,
                                  _SAMPLE_CFG["pages_per_step"])
            return o.astype(BF16)

        ys, k_s, v_s, last_s, wkv_s = _layer(ys, bs, attn_s, state_wkv[l], _pad_feat(state_shift[l]),
                                   cache_mem_k[l].reshape(bs, MEM_LEN, MEM_WIDTH),
                                   cache_mem_v[l].reshape(bs, MEM_LEN, MEM_WIDTH), p, _SAMPLE_CFG)

        outs[0].append(k_p.reshape(bp, tp, DA_HEADS, DA_HV))
        outs[1].append(v_p.reshape(bp, tp, DA_HEADS, DA_HV))
        outs[2].append(wkv_p)
        outs[3].append(_unpad_feat(last_p))
        outs[4].append(mk_p.reshape(bp, MEM_LEN, MEM_HEADS, MEM_DH))
        outs[5].append(mv_p.reshape(bp, MEM_LEN, MEM_HEADS, MEM_DH))
        outs[6].append(k_s.reshape(bs, ts, DA_HEADS, DA_HV))
        outs[7].append(v_s.reshape(bs, ts, DA_HEADS, DA_HV))
        outs[8].append(wkv_s)
        outs[9].append(_unpad_feat(last_s))
    return (yp.reshape(bp, tp, d), ys.reshape(bs, ts, d)) + tuple(jnp.stack(o) for o in outs)
```

```python
import functools
import math

import jax
import jax.numpy as jnp
import numpy as np
from jax import lax
from jax.experimental import pallas as pl
from jax.experimental.pallas import tpu as pltpu

F32 = jnp.float32
BF16 = jnp.bfloat16

D_MODEL = 4096
DA_HEADS = 8
DA_DH = 128
DA_HV = 2 * DA_DH
DA_WIDTH = DA_HEADS * DA_HV
RW_HEAD = 64
RW_WIDTH = 2048
RW_HEADS = RW_WIDTH // RW_HEAD
DECAY_LORA = 96
AAA_LORA = 96
GATE_LORA = 256
RW_PROJ = 3 * RW_WIDTH + DECAY_LORA + AAA_LORA + GATE_LORA
LORA_PAD = 128
LORA_BLOCK = 2 * LORA_PAD + GATE_LORA
FEAT_PAD = 3 * RW_WIDTH + LORA_BLOCK
MEM_LEN = 256
MEM_HEADS = 4
MEM_DH = 128
MEM_WIDTH = MEM_HEADS * MEM_DH
PAGE_SIZE = 128
DEPTH = 1
LN_EPS = 1e-5
SUBLN_EPS = 1e-5
RW_GN_EPS = 64e-5
ALPHA = (2.0 * DEPTH) ** 0.25

LANES = 128
VMEM_LIMIT = 56 * 1024 * 1024
NEG_INF = float("-inf")

_NT = (((1,), (1,)), ((), ()))
LOG2E = math.log2(math.e)
QK_SCALE_LOG2 = DA_DH ** -0.5 * LOG2E


def _cparams(*sem):
    return pltpu.CompilerParams(dimension_semantics=sem, vmem_limit_bytes=VMEM_LIMIT)


def _layernorm(x, g, b):
    mu = jnp.mean(x, axis=-1, keepdims=True)
    d = x - mu
    var = jnp.mean(d * d, axis=-1, keepdims=True)
    return d * lax.rsqrt(var + LN_EPS) * g + b


def _sigmoid(x):
    return 1.0 / (1.0 + jnp.exp(-x))


def _split2(x):
    hi = x.astype(BF16)
    lo = (x - hi.astype(F32)).astype(BF16)
    return hi, lo


def _segsum(x, jj_ref):
    hi, lo = _split2(x)
    return jnp.dot(jnp.concatenate([hi, lo], axis=1), jj_ref[...], preferred_element_type=F32)


def _seg_ones(parts):
    r = np.arange(parts * LANES)[:, None] % LANES
    c = np.arange(LANES)[None, :]
    return jnp.asarray((r // RW_HEAD) == (c // RW_HEAD), dtype=BF16)


def _diag_mask():
    r = np.arange(RW_HEAD)[:, None]
    c = np.arange(LANES)[None, :]
    return jnp.asarray((c % RW_HEAD) == r, dtype=F32)


def _mm_kernel(a_ref, w_ref, o_ref):
    o_ref[...] = jnp.dot(a_ref[...], w_ref[...], preferred_element_type=F32)


def _matmul(a, w, tm, tn, col0=0, ncols=None):
    m, k = a.shape
    n = w.shape[1] if ncols is None else ncols
    assert m % tm == 0 and n % tn == 0 and col0 % tn == 0
    j0 = col0 // tn
    return pl.pallas_call(
        _mm_kernel,
        grid=(m // tm, n // tn),
        in_specs=[pl.BlockSpec((tm, k), lambda i, j: (i, 0)),
                  pl.BlockSpec((k, tn), lambda i, j: (0, j0 + j))],
        out_specs=pl.BlockSpec((tm, tn), lambda i, j: (i, j)),
        out_shape=jax.ShapeDtypeStruct((m, n), F32),
        compiler_params=_cparams("parallel", "arbitrary"),
        name="matmul",
    )(a, w)


def _lambda(lamv_ref, lam_init):
    lv = lamv_ref[...]
    s1 = jnp.sum(lv[0:1] * lv[1:2], axis=-1, keepdims=True)
    s2 = jnp.sum(lv[2:3] * lv[3:4], axis=-1, keepdims=True)
    return jnp.exp(s1) - jnp.exp(s2) + lam_init


def _diff_combine(acc_ref, l_ref, i1, i2, lam, g, lam_init):
    o = acc_ref[i1] / l_ref[i1] - lam * (acc_ref[i2] / l_ref[i2])
    o = o * lax.rsqrt(jnp.mean(o * o, axis=-1, keepdims=True) + SUBLN_EPS) * g
    return o * (1.0 - lam_init)


def _pattn_kernel(qi_ref, ki_ref, slopes_ref, lamv_ref, g_ref, q_ref, k_ref, v_ref, o_ref,
                  qs_scr, m_scr, l_scr, acc_scr, *, lam_init, tq, rc):
    h = pl.program_id(0)
    st = pl.program_id(1)
    qb = qi_ref[st]
    kb = ki_ref[st]
    slope = slopes_ref[h]

    @pl.when(kb == 0)
    def _init():
        qs_scr[...] = (q_ref[...] * QK_SCALE_LOG2).astype(BF16)
        m_scr[...] = jnp.full(m_scr.shape, NEG_INF, F32)
        l_scr[...] = jnp.zeros(l_scr.shape, F32)
        acc_scr[...] = jnp.zeros(acc_scr.shape, F32)

    def _step(masked):
        k = k_ref[...].astype(BF16)
        v = v_ref[...].astype(BF16)
        col = lax.broadcasted_iota(jnp.int32, (1, tq), 1)
        bias = slope * ((kb - qb) * tq + col).astype(F32)
        pieces = [(c, r0) for r0 in range(0, tq, rc) for c in range(2)]

        def scores(c, r0):
            s = lax.dot_general(qs_scr[r0:r0 + rc, c * DA_DH:(c + 1) * DA_DH], k[:, c * DA_DH:(c + 1) * DA_DH],
                                _NT, preferred_element_type=F32) + bias
            if masked:
                row2 = r0 + lax.broadcasted_iota(jnp.int32, (rc, tq), 0)
                col2 = lax.broadcasted_iota(jnp.int32, (rc, tq), 1)
                s = jnp.where(col2 > row2, NEG_INF, s)
            return s

        def softmax(c, r0, s):
            m_old = m_scr[c, r0:r0 + rc]
            m_new = jnp.maximum(m_old, jnp.max(s, axis=-1, keepdims=True))
            corr = jnp.exp2(m_old - m_new)
            p = jnp.exp2(s - m_new)
            l_scr[c, r0:r0 + rc] = l_scr[c, r0:r0 + rc] * corr + jnp.sum(p, axis=-1, keepdims=True)
            m_scr[c, r0:r0 + rc] = m_new
            return p.astype(BF16), corr

        def values(c, r0, p, corr):
            acc_scr[c, r0:r0 + rc] = acc_scr[c, r0:r0 + rc] * corr + jnp.dot(p, v, preferred_element_type=F32)

        n = len(pieces)
        s_q, p_q = {}, {}
        for i in range(n + 2):
            if i < n:
                s_q[i] = scores(*pieces[i])
            if 1 <= i <= n:
                p_q[i - 1] = softmax(*pieces[i - 1], s_q.pop(i - 1))
            if i >= 2:
                values(*pieces[i - 2], *p_q.pop(i - 2))

    @pl.when(kb < qb)
    def _off_diag():
        _step(False)

    @pl.when(kb == qb)
    def _diag():
        _step(True)
        lam = _lambda(lamv_ref, lam_init)
        o_ref[...] = _diff_combine(acc_scr, l_scr, 0, 1, lam, g_ref[...], lam_init).astype(o_ref.dtype)


def _prompt_attention(q, k, v, lamv, subln_g, lam_init, tq, rc):
    t = q.shape[0]
    assert t % tq == 0 and tq % rc == 0
    nq = t // tq
    qi = np.concatenate([np.full(q + 1, q, np.int32) for q in range(nq)])
    ki = np.concatenate([np.arange(q + 1, dtype=np.int32) for q in range(nq)])
    slopes = jnp.asarray(LOG2E * 2.0 ** (-8.0 * np.arange(1, DA_HEADS + 1) / DA_HEADS), F32)
    grid_spec = pltpu.PrefetchScalarGridSpec(
        num_scalar_prefetch=3,
        grid=(DA_HEADS, len(qi)),
        in_specs=[
            pl.BlockSpec((4, DA_DH), lambda h, s, qi, ki, sl: (0, 0)),
            pl.BlockSpec((1, DA_HV), lambda h, s, qi, ki, sl: (0, 0)),
            pl.BlockSpec((tq, DA_HV), lambda h, s, qi, ki, sl: (qi[s], h)),
            pl.BlockSpec((tq, DA_HV), lambda h, s, qi, ki, sl: (ki[s], h)),
            pl.BlockSpec((tq, DA_HV), lambda h, s, qi, ki, sl: (ki[s], h)),
        ],
        out_specs=pl.BlockSpec((tq, DA_HV), lambda h, s, qi, ki, sl: (qi[s], h)),
        scratch_shapes=[
            pltpu.VMEM((tq, DA_HV), BF16),
            pltpu.VMEM((2, tq, 1), F32),
            pltpu.VMEM((2, tq, 1), F32),
            pltpu.VMEM((2, tq, DA_HV), F32),
        ],
    )
    return pl.pallas_call(
        functools.partial(_pattn_kernel, lam_init=lam_init, tq=tq, rc=rc),
        grid_spec=grid_spec,
        out_shape=jax.ShapeDtypeStruct((t, DA_WIDTH), BF16),
        compiler_params=_cparams("parallel", "arbitrary"),
        name="prompt_attention",
    )(jnp.asarray(qi), jnp.asarray(ki), slopes, lamv, subln_g, q, k, v)


def _sattn_kernel(pt_ref, lamv_ref, g_ref, q_ref, kn_ref, vn_ref, *rest, pages_per_step, lam_init):
    pp = pages_per_step
    kp_refs = rest[:pp]
    vp_refs = rest[pp:2 * pp]
    o_ref = rest[2 * pp]
    qs_scr, bias_scr, m_scr, l_scr, acc_scr = rest[2 * pp + 1:]
    st = pl.program_id(1)
    n_steps = pl.num_programs(1)
    rows = q_ref.shape[0]
    cols = PAGE_SIZE * DA_HEADS
    past_len = n_steps * pp * PAGE_SIZE

    head_r = lax.broadcasted_iota(jnp.int32, (rows, 1), 0) % DA_HEADS
    slope = LOG2E * jnp.exp2((head_r + 1).astype(F32) * (-8.0 / DA_HEADS))

    @pl.when(st == 0)
    def _init():
        qs_scr[...] = (q_ref[...] * QK_SCALE_LOG2).astype(BF16)
        lane = lax.broadcasted_iota(jnp.int32, (rows, cols), 1)
        row = lax.broadcasted_iota(jnp.int32, (rows, cols), 0)
        same_head = (lane % DA_HEADS) == (row % DA_HEADS)
        bias_scr[...] = jnp.where(same_head, slope * (lane // DA_HEADS).astype(F32), NEG_INF)
        m_scr[...] = jnp.full(m_scr.shape, NEG_INF, F32)
        l_scr[...] = jnp.zeros(l_scr.shape, F32)
        acc_scr[...] = jnp.zeros(acc_scr.shape, F32)

    def _blocks(k_refs, v_refs, biases):
        ks = [r[...].astype(BF16) for r in k_refs]
        vs = [r[...].astype(BF16) for r in v_refs]
        ps, corrs = [], []
        for c in range(2):
            q = qs_scr[:, c * DA_DH:(c + 1) * DA_DH]
            ss = [lax.dot_general(q, k[:, c * DA_DH:(c + 1) * DA_DH], _NT, preferred_element_type=F32) + b
                  for k, b in zip(ks, biases)]
            m_old = m_scr[c]
            m_new = functools.reduce(jnp.maximum, [jnp.max(s, axis=-1, keepdims=True) for s in ss], m_old)
            corr = jnp.exp2(m_old - m_new)
            pc = [jnp.exp2(s - m_new) for s in ss]
            l_scr[c] = l_scr[c] * corr + functools.reduce(
                lambda x, y: x + y, [jnp.sum(p, axis=-1, keepdims=True) for p in pc])
            m_scr[c] = m_new
            ps.append([p.astype(BF16) for p in pc])
            corrs.append(corr)
        pv = functools.reduce(lambda x, y: x + y, [
            jnp.dot(jnp.concatenate([ps[0][i], ps[1][i]], axis=0), vs[i], preferred_element_type=F32)
            for i in range(len(vs))])
        for c in range(2):
            acc_scr[c] = acc_scr[c] * corrs[c] + pv[c * rows:(c + 1) * rows]

    base = bias_scr[...]
    _blocks(kp_refs, vp_refs,
            [base + slope * ((st * pp + p) * PAGE_SIZE - past_len).astype(F32) for p in range(pp)])

    @pl.when(st == n_steps - 1)
    def _last():
        lane = lax.broadcasted_iota(jnp.int32, (rows, cols), 1)
        row = lax.broadcasted_iota(jnp.int32, (rows, cols), 0)
        causal = (lane // DA_HEADS) <= (row // DA_HEADS)
        _blocks([kn_ref], [vn_ref], [jnp.where(causal, bias_scr[...], NEG_INF)])
        lam = _lambda(lamv_ref, lam_init)
        o_ref[...] = _diff_combine(acc_scr, l_scr, 0, 1, lam, g_ref[...], lam_init).astype(o_ref.dtype)


def _sample_attention(q2d, k2d, v2d, cache_k, cache_v, page_table, lamv, subln_g, lam_init, n_tok,
                      pages_per_step):
    b, n_pages = page_table.shape
    pp = pages_per_step
    assert n_pages % pp == 0
    n_phys = cache_k.shape[0]
    rows = n_tok * DA_HEADS
    cols = PAGE_SIZE * DA_HEADS
    ck = cache_k.reshape(n_phys, cols, DA_HV)
    cv = cache_v.reshape(n_phys, cols, DA_HV)
    q = q2d.reshape(b, rows, DA_HV)
    pad = ((0, 0), (0, PAGE_SIZE - n_tok), (0, 0))
    k_new = jnp.pad(k2d.reshape(b, n_tok, DA_WIDTH), pad).reshape(b, cols, DA_HV)
    v_new = jnp.pad(v2d.reshape(b, n_tok, DA_WIDTH), pad).reshape(b, cols, DA_HV)

    def page_spec(p):
        return pl.BlockSpec((None, cols, DA_HV), lambda bi, s, pt, p=p: (pt[bi * n_pages + s * pp + p], 0, 0))

    grid_spec = pltpu.PrefetchScalarGridSpec(
        num_scalar_prefetch=1,
        grid=(b, n_pages // pp),
        in_specs=[
            pl.BlockSpec((4, DA_DH), lambda bi, s, pt: (0, 0)),
            pl.BlockSpec((1, DA_HV), lambda bi, s, pt: (0, 0)),
            pl.BlockSpec((None, rows, DA_HV), lambda bi, s, pt: (bi, 0, 0)),
            pl.BlockSpec((None, cols, DA_HV), lambda bi, s, pt: (bi, 0, 0)),
            pl.BlockSpec((None, cols, DA_HV), lambda bi, s, pt: (bi, 0, 0)),
        ] + [page_spec(p) for p in range(pp)] + [page_spec(p) for p in range(pp)],
        out_specs=pl.BlockSpec((None, rows, DA_HV), lambda bi, s, pt: (bi, 0, 0)),
        scratch_shapes=[
            pltpu.VMEM((rows, DA_HV), BF16),
            pltpu.VMEM((rows, cols), F32),
            pltpu.VMEM((2, rows, 1), F32),
            pltpu.VMEM((2, rows, 1), F32),
            pltpu.VMEM((2, rows, DA_HV), F32),
        ],
    )
    out = pl.pallas_call(
        functools.partial(_sattn_kernel, pages_per_step=pp, lam_init=lam_init),
        grid_spec=grid_spec,
        out_shape=jax.ShapeDtypeStruct((b, rows, DA_HV), F32),
        compiler_params=_cparams("parallel", "arbitrary"),
        name="sample_attention",
    )(page_table.reshape(-1), lamv, subln_g, q, k_new, v_new, *([ck] * pp), *([cv] * pp))
    return out.reshape(b * n_tok, DA_WIDTH)


def _rwkv_prep_kernel(fr_ref, fk_ref, fv_ref, fl_ref, pr_ref, pk_ref, pv_ref, pl_ref,
                      mur_ref, muk_ref, muv_ref, mul_ref, w0_ref, a0_ref, kkw_ref, kaw_ref,
                      w2_ref, a2_ref, g2_ref, jj_ref,
                      r_out, w_out, k_out, v_out, kk_out, b_out, g_out,
                      cr_scr, ck_scr, cv_scr, cl_scr):
    tb = pl.program_id(1)
    tm = fr_ref.shape[0]

    @pl.when(tb == 0)
    def _load_state():
        cr_scr[...] = pr_ref[...]
        ck_scr[...] = pk_ref[...]
        cv_scr[...] = pv_ref[...]
        cl_scr[...] = pl_ref[...]

    def mixed(f_ref, carry_scr, mu_ref):
        x = f_ref[...]
        row = lax.broadcasted_iota(jnp.int32, x.shape, 0)
        shifted = jnp.where(row == 0, carry_scr[...], pltpu.roll(x, 1, 0))
        carry_scr[...] = x[tm - 1:tm, :]
        return x + (shifted - x) * mu_ref[...]

    xr = mixed(fr_ref, cr_scr, mur_ref)
    xk = mixed(fk_ref, ck_scr, muk_ref)
    xv = mixed(fv_ref, cv_scr, muv_ref)
    xl = mixed(fl_ref, cl_scr, mul_ref)

    dw = jnp.tanh(xl[:, 0:LORA_PAD]).astype(BF16)
    da = xl[:, LORA_PAD:2 * LORA_PAD].astype(BF16)
    dg = _sigmoid(xl[:, 2 * LORA_PAD:]).astype(BF16)
    wl = w0_ref[...] + jnp.dot(dw, w2_ref[...], preferred_element_type=F32)
    neg = -wl
    softplus = jnp.maximum(neg, 0.0) + jnp.log(1.0 + jnp.exp(-jnp.abs(neg)))
    log_w = -jnp.exp(-softplus - 0.5)
    a = _sigmoid(a0_ref[...] + jnp.dot(da, a2_ref[...], preferred_element_type=F32))
    g = jnp.dot(dg, g2_ref[...], preferred_element_type=F32)

    kk0 = xk * kkw_ref[...]
    sq = kk0 * kk0
    ss = jnp.concatenate([_segsum(sq[:, n * LANES:(n + 1) * LANES], jj_ref)
                          for n in range(RW_WIDTH // LANES)], axis=1)
    kk = kk0 / jnp.maximum(jnp.sqrt(ss), 1e-12)

    r_out[...] = xr
    w_out[...] = log_w
    k_out[...] = xk * (1.0 + (a - 1.0) * kaw_ref[...])
    v_out[...] = xv
    kk_out[...] = kk
    b_out[...] = kk * a
    g_out[...] = g


def _rwkv_prep(proj, lora, prev_pad, mu_pad, w0, a0, k_k, k_a, w2p, a2p, g2b, n_batch, tm):
    m = proj.shape[0]
    t = m // n_batch
    assert t % tm == 0
    nt = t // tm
    base = 0
    prev3 = prev_pad.reshape(n_batch, 1, FEAT_PAD)
    mu2 = mu_pad.reshape(1, FEAT_PAD)

    def feat(sec):
        return pl.BlockSpec((tm, RW_WIDTH), lambda b, i, sec=sec: (b * nt + i, base + sec))

    def prev(sec):
        return pl.BlockSpec((None, 1, RW_WIDTH), lambda b, i, sec=sec: (b, 0, sec))

    def mu(sec):
        return pl.BlockSpec((1, RW_WIDTH), lambda b, i, sec=sec: (0, sec))

    vec = pl.BlockSpec((1, RW_WIDTH), lambda b, i: (0, 0))
    out = pl.BlockSpec((tm, RW_WIDTH), lambda b, i: (b * nt + i, 0))
    in_specs = [
        feat(0), feat(1), feat(2),
        pl.BlockSpec((tm, LORA_BLOCK), lambda b, i: (b * nt + i, 0)),
        prev(0), prev(1), prev(2),
        pl.BlockSpec((None, 1, LORA_BLOCK), lambda b, i: (b, 0, 3 * RW_WIDTH // LORA_BLOCK)),
        mu(0), mu(1), mu(2),
        pl.BlockSpec((1, LORA_BLOCK), lambda b, i: (0, 3 * RW_WIDTH // LORA_BLOCK)),
        vec, vec, vec, vec,
        pl.BlockSpec((LORA_PAD, RW_WIDTH), lambda b, i: (0, 0)),
        pl.BlockSpec((LORA_PAD, RW_WIDTH), lambda b, i: (0, 0)),
        pl.BlockSpec((GATE_LORA, RW_WIDTH), lambda b, i: (0, 0)),
        pl.BlockSpec((2 * LANES, LANES), lambda b, i: (0, 0)),
    ]
    shp = jax.ShapeDtypeStruct((m, RW_WIDTH), F32)
    return pl.pallas_call(
        _rwkv_prep_kernel,
        grid=(n_batch, nt),
        in_specs=in_specs,
        out_specs=[out] * 7,
        out_shape=[shp] * 7,
        scratch_shapes=[pltpu.VMEM((1, RW_WIDTH), F32)] * 3 + [pltpu.VMEM((1, LORA_BLOCK), F32)],
        compiler_params=_cparams("parallel", "arbitrary"),
        name="rwkv_prep",
    )(proj, proj, proj, lora, prev3, prev3, prev3, prev3, mu2, mu2, mu2, mu2,
      w0.reshape(1, -1), a0.reshape(1, -1), k_k.reshape(1, -1), k_a.reshape(1, -1),
      w2p, a2p, g2b, _seg_ones(2))


N_TILES = RW_WIDTH // LANES
S_ROWS = N_TILES * RW_HEAD


def _wkv_kernel(r_ref, w_ref, k_ref, v_ref, kk_ref, b_ref, g_ref, s0_ref,
                lng_ref, lnb_ref, rk_ref, jj2_ref, jj3_ref, dm_ref,
                o_ref, st_ref, s_scr, o_scr):
    tb = pl.program_id(1)
    n_tb = pl.num_programs(1)
    t_blk = r_ref.shape[0]

    @pl.when(tb == 0)
    def _load_state():
        s_scr[...] = s0_ref[...]

    dm = dm_ref[...]
    dm_all = jnp.concatenate([dm] * N_TILES, axis=0)
    dm_all_b = dm_all.astype(BF16)

    def expand(row):
        return jnp.concatenate(
            [jnp.broadcast_to(row[:, n * LANES:(n + 1) * LANES], (RW_HEAD, LANES)) for n in range(N_TILES)],
            axis=0)

    def step(t, carry):
        s = s_scr[...]
        kkx = expand(kk_ref[pl.ds(t, 1), :])
        wx = expand(jnp.exp(w_ref[pl.ds(t, 1), :]))
        bx = expand(b_ref[pl.ds(t, 1), :])
        kx = expand(k_ref[pl.ds(t, 1), :])
        rx = expand(r_ref[pl.ds(t, 1), :])
        sa = -_segsum(s * kkx, jj2_ref)
        v_row = v_ref[pl.ds(t, 1), :]
        v_hi = v_row.astype(BF16)
        rem = v_row - v_hi.astype(F32)
        v_mid = rem.astype(BF16)
        v_lo = (rem - v_mid.astype(F32)).astype(BF16)
        v_parts = jnp.concatenate([expand(p) * dm_all_b for p in (v_hi, v_mid, v_lo)], axis=1)
        vx = jnp.dot(v_parts, jj3_ref[...], preferred_element_type=F32)
        s_new = s * wx + sa * bx + vx * kx
        s_scr[...] = s_new
        ox = _segsum(s_new * rx, jj2_ref) * dm_all
        o_row = jnp.concatenate(
            [jnp.sum(ox[n * RW_HEAD:(n + 1) * RW_HEAD], axis=0, keepdims=True) for n in range(N_TILES)], axis=1)
        o_scr[pl.ds(t, 1), :] = o_row
        return carry

    lax.fori_loop(0, t_blk, step, 0)

    for n in range(N_TILES):
        sl = slice(n * LANES, (n + 1) * LANES)
        o = o_scr[:, sl]
        mu = _segsum(o, jj2_ref) * (1.0 / RW_HEAD)
        d = o - mu
        var = _segsum(d * d, jj2_ref) * (1.0 / RW_HEAD)
        on = d * lax.rsqrt(var + RW_GN_EPS) * lng_ref[:, sl] + lnb_ref[:, sl]
        bonus = _segsum(r_ref[:, sl] * k_ref[:, sl] * rk_ref[:, sl], jj2_ref) * v_ref[:, sl]
        o_ref[:, sl] = ((on + bonus) * g_ref[:, sl]).astype(o_ref.dtype)

    @pl.when(tb == n_tb - 1)
    def _store_state():
        st_ref[...] = s_scr[...]


def _state_to_tiles(s):
    b = s.shape[0]
    return s.reshape(b, N_TILES, 2, RW_HEAD, RW_HEAD).transpose(0, 1, 3, 2, 4).reshape(b, S_ROWS, LANES)


def _tiles_to_state(s):
    b = s.shape[0]
    return s.reshape(b, N_TILES, RW_HEAD, 2, RW_HEAD).transpose(0, 1, 3, 2, 4).reshape(
        b, RW_HEADS, RW_HEAD, RW_HEAD)


def _wkv(feats, s0_tiles, lnx_g, lnx_b, r_k, n_batch, t_blk):
    r, w, k, v, kk, bb, g = feats
    m = r.shape[0]
    t = m // n_batch
    assert t % t_blk == 0
    nt = t // t_blk
    blk = pl.BlockSpec((t_blk, RW_WIDTH), lambda b, i: (b * nt + i, 0))
    vec = pl.BlockSpec((1, RW_WIDTH), lambda b, i: (0, 0))
    st = pl.BlockSpec((None, S_ROWS, LANES), lambda b, i: (b, 0, 0))
    return pl.pallas_call(
        _wkv_kernel,
        grid=(n_batch, nt),
        in_specs=[blk] * 7 + [st, vec, vec, vec,
                              pl.BlockSpec((2 * LANES, LANES), lambda b, i: (0, 0)),
                              pl.BlockSpec((3 * LANES, LANES), lambda b, i: (0, 0)),
                              pl.BlockSpec((RW_HEAD, LANES), lambda b, i: (0, 0))],
        out_specs=[blk, st],
        out_shape=[jax.ShapeDtypeStruct((m, RW_WIDTH), BF16),
                   jax.ShapeDtypeStruct((n_batch, S_ROWS, LANES), F32)],
        scratch_shapes=[pltpu.VMEM((S_ROWS, LANES), F32), pltpu.VMEM((t_blk, RW_WIDTH), F32)],
        compiler_params=_cparams("parallel", "arbitrary"),
        name="wkv",
    )(r, w, k, v, kk, bb, g, s0_tiles, lnx_g.reshape(1, -1), lnx_b.reshape(1, -1), r_k.reshape(1, -1),
      _seg_ones(2), _seg_ones(3), _diag_mask())


_NN = (((1,), (0,)), ((), ()))


def _mm3(a, b, dims=_NN):
    zero = jnp.zeros_like(b[0])
    lhs = jnp.concatenate([a[0], a[1]], axis=1)
    if dims == _NN:
        rhs = jnp.concatenate([jnp.concatenate([b[0], b[1]], axis=1), jnp.concatenate([b[0], zero], axis=1)], axis=0)
    else:
        rhs = jnp.concatenate([jnp.concatenate([b[0], b[0]], axis=1), jnp.concatenate([b[1], zero], axis=1)], axis=0)
    out = lax.dot_general(lhs, rhs, dims, preferred_element_type=F32)
    n = out.shape[1] // 2
    return out[:, :n] + out[:, n:]


def _wkv_chunk_tiles(lw, r, k, v, kk, b, ys_in, c_len):
    rows = 2 * c_len
    lane = lax.broadcasted_iota(jnp.int32, (1, LANES), 1)
    head_a = (lane < RW_HEAD).astype(F32)
    head_b = 1.0 - head_a

    def stack(x):
        return jnp.concatenate([x * head_a, x * head_b], axis=0)

    def each(fn, *lists):
        return [fn(*args) for args in zip(*lists)]

    tr = lax.broadcasted_iota(jnp.int32, (c_len, c_len), 0)
    tc = lax.broadcasted_iota(jnp.int32, (c_len, c_len), 1)
    tril = (tc <= tr).astype(BF16)

    def cumsum(x):
        l1 = x.astype(BF16)
        rem = x - l1.astype(F32)
        l2 = rem.astype(BF16)
        l3 = (rem - l2.astype(F32)).astype(BF16)
        return (jnp.dot(tril, l1, preferred_element_type=F32) + jnp.dot(tril, l2, preferred_element_type=F32)
                + jnp.dot(tril, l3, preferred_element_type=F32))

    cs = each(cumsum, lw)
    cs_last = [c[c_len - 1:c_len, :] for c in cs]
    g_rem = each(lambda c, cl: jnp.exp(cl - c), cs, cs_last)
    g_inv = each(lambda c: jnp.exp(-c), cs)
    a2f = each(lambda x, c, l: stack(-x * jnp.exp(c - l)), kk, cs, lw)
    r2f = each(lambda x, c: stack(x * jnp.exp(c)), r, cs)
    v2f = each(stack, v)
    a2 = each(_split2, a2f)
    r2 = each(_split2, r2f)
    v2 = each(_split2, v2f)
    b2 = each(lambda x, g: _split2(stack(x * g)), b, g_inv)
    k2 = each(lambda x, g: _split2(stack(x * g)), k, g_inv)
    bg2 = each(lambda x, g: _split2(stack(x * g)), b, g_rem)
    kg2 = each(lambda x, g: _split2(stack(x * g)), k, g_rem)

    rr = lax.broadcasted_iota(jnp.int32, (rows, rows), 0)
    cc = lax.broadcasted_iota(jnp.int32, (rows, rows), 1)
    same = (rr // c_len) == (cc // c_len)
    strict = same & ((cc % c_len) < (rr % c_len))
    lower = same & ((cc % c_len) <= (rr % c_len))
    eye = jnp.where(rr == cc, 1.0, 0.0)

    ar2 = each(lambda x, y: (jnp.concatenate([x[0], y[0]], axis=0), jnp.concatenate([x[1], y[1]], axis=0)), a2, r2)
    gb = each(lambda x, y: _mm3(x, y, _NT), ar2, b2)
    gk = each(lambda x, y: _mm3(x, y, _NT), ar2, k2)
    l_mat = [jnp.where(strict, x[:rows], 0.0) for x in gb]
    ak = [jnp.where(strict, x[:rows], 0.0) for x in gk]
    rb = [jnp.where(lower, x[rows:], 0.0) for x in gb]
    rk = [jnp.where(lower, x[rows:], 0.0) for x in gk]

    t_mat = [eye + x for x in l_mat]
    p_mat = l_mat
    for _ in range(max(int(math.log2(c_len)) - 1, 0)):
        ps = each(_split2, p_mat)
        p_mat = each(lambda x: _mm3(x, x), ps)
        t_mat = each(lambda t, p: t + _mm3(_split2(t), _split2(p)), t_mat, p_mat)

    akv = each(lambda x, y: _mm3(_split2(x), y), ak, v2)
    wu = each(lambda t, x, y: _mm3(_split2(t), _split2(jnp.concatenate([x, y], axis=1))),
              t_mat, a2f, akv)
    one = lambda x, y, dims=_NN: lax.dot_general(x.astype(BF16), y.astype(BF16), dims, preferred_element_type=F32)
    qo = each(one, rb, wu)
    q2 = each(lambda x, y: x + y[:, :LANES], r2f, qo)
    o0 = each(lambda q, x, y: q[:, LANES:] + one(x, y), qo, rk, v2f)
    mn = each(lambda x, y: _mm3(_split2(x.T), y), wu, bg2)
    n_mat = each(lambda m, x, y: m[LANES:] + _mm3(_split2(x.T), y), mn, v2f, kg2)

    ys = each(_split2, ys_in)
    o2 = each(lambda q, y, o: one(q, y, _NT) + o, q2, ys_in, o0)
    o = [x[:c_len] + x[c_len:] for x in o2]
    y_new = each(lambda y, cl, ysp, m, nm: y * jnp.exp(cl) + _mm3(ysp, _split2(m[:LANES])) + nm,
                 ys_in, cs_last, ys, mn, n_mat)
    return o, y_new


def _wkv_chunk_kernel(lw_ref, r_ref, k_ref, v_ref, kk_ref, b_ref, g_ref, y0_ref,
                      lng_ref, lnb_ref, rk_ref, jj2_ref, o_ref, yt_ref, y_scr):
    c = pl.program_id(2)
    n_c = pl.num_programs(2)
    c_len = r_ref.shape[0]
    tiles = r_ref.shape[1] // LANES

    @pl.when(c == 0)
    def _load_state():
        y_scr[...] = y0_ref[...]

    sls = [slice(t * LANES, (t + 1) * LANES) for t in range(tiles)]
    split = lambda ref: [ref[:, sl] for sl in sls]
    r, k, v = split(r_ref), split(k_ref), split(v_ref)
    o, y_new = _wkv_chunk_tiles(split(lw_ref), r, k, v, split(kk_ref), split(b_ref),
                                [y_scr[t] for t in range(tiles)], c_len)
    for t, sl in enumerate(sls):
        y_scr[t] = y_new[t]
        mu = _segsum(o[t], jj2_ref) * (1.0 / RW_HEAD)
        d = o[t] - mu
        var = _segsum(d * d, jj2_ref) * (1.0 / RW_HEAD)
        on = d * lax.rsqrt(var + RW_GN_EPS) * lng_ref[:, sl] + lnb_ref[:, sl]
        bonus = _segsum(r[t] * k[t] * rk_ref[:, sl], jj2_ref) * v[t]
        o_ref[:, sl] = ((on + bonus) * g_ref[:, sl]).astype(o_ref.dtype)

    @pl.when(c == n_c - 1)
    def _store_state():
        yt_ref[...] = y_scr[...]


def _state_to_blockdiag(s):
    b = s.shape[0]
    s5 = s.reshape(b, N_TILES, 2, RW_HEAD, RW_HEAD)
    z = jnp.zeros((b, N_TILES, RW_HEAD, RW_HEAD), s.dtype)
    top = jnp.concatenate([s5[:, :, 0], z], axis=-1)
    bot = jnp.concatenate([z, s5[:, :, 1]], axis=-1)
    return jnp.concatenate([top, bot], axis=-2)


def _blockdiag_to_state(y):
    b = y.shape[0]
    s = jnp.stack([y[:, :, :RW_HEAD, :RW_HEAD], y[:, :, RW_HEAD:, RW_HEAD:]], axis=2)
    return s.reshape(b, RW_HEADS, RW_HEAD, RW_HEAD)


def _wkv_chunked(feats, y0, lnx_g, lnx_b, r_k, n_batch, c_len, tiles):
    r, lw, k, v, kk, bb, g = feats
    m = r.shape[0]
    t = m // n_batch
    assert t % c_len == 0 and N_TILES % tiles == 0
    nc = t // c_len
    wid = tiles * LANES
    blk = pl.BlockSpec((c_len, wid), lambda b, gi, c: (b * nc + c, gi))
    vec = pl.BlockSpec((1, wid), lambda b, gi, c: (0, gi))
    st = pl.BlockSpec((None, tiles, LANES, LANES), lambda b, gi, c: (b, gi, 0, 0))
    return pl.pallas_call(
        _wkv_chunk_kernel,
        grid=(n_batch, N_TILES // tiles, nc),
        in_specs=[blk] * 7 + [st, vec, vec, vec, pl.BlockSpec((2 * LANES, LANES), lambda b, gi, c: (0, 0))],
        out_specs=[blk, st],
        out_shape=[jax.ShapeDtypeStruct((m, RW_WIDTH), BF16),
                   jax.ShapeDtypeStruct((n_batch, N_TILES, LANES, LANES), F32)],
        scratch_shapes=[pltpu.VMEM((tiles, LANES, LANES), F32)],
        compiler_params=_cparams("parallel", "parallel", "arbitrary"),
        name="wkv_chunked",
    )(lw, r, k, v, kk, bb, g, y0, lnx_g.reshape(1, -1), lnx_b.reshape(1, -1), r_k.reshape(1, -1), _seg_ones(2))


def _outproj_ln_kernel(a1_ref, a2_ref, w_ref, res_ref, g_ref, b_ref, o_ref, acc_scr):
    j = pl.program_id(1)
    nn = pl.num_programs(1)
    k1 = a1_ref.shape[1]
    mix = (jnp.dot(a1_ref[...], w_ref[:k1, :], preferred_element_type=F32)
           + jnp.dot(a2_ref[...], w_ref[k1:, :], preferred_element_type=F32))
    acc_scr[j] = ALPHA * res_ref[...] + mix

    @pl.when(j == nn - 1)
    def _finish():
        y = jnp.concatenate([acc_scr[n] for n in range(acc_scr.shape[0])], axis=1)
        o_ref[...] = _layernorm(y, g_ref[...], b_ref[...])


def _outproj_ln(a1, a2, w, res, g, b, tm, tn):
    m, k1 = a1.shape
    k2 = a2.shape[1]
    n = w.shape[1]
    assert m % tm == 0 and n % tn == 0
    return pl.pallas_call(
        _outproj_ln_kernel,
        grid=(m // tm, n // tn),
        in_specs=[pl.BlockSpec((tm, k1), lambda i, j: (i, 0)),
                  pl.BlockSpec((tm, k2), lambda i, j: (i, 0)),
                  pl.BlockSpec((k1 + k2, tn), lambda i, j: (0, j)),
                  pl.BlockSpec((tm, tn), lambda i, j: (i, j)),
                  pl.BlockSpec((1, n), lambda i, j: (0, 0)),
                  pl.BlockSpec((1, n), lambda i, j: (0, 0))],
        out_specs=pl.BlockSpec((tm, n), lambda i, j: (i, 0)),
        out_shape=jax.ShapeDtypeStruct((m, n), F32),
        scratch_shapes=[pltpu.VMEM((n // tn, tm, tn), F32)],
        compiler_params=_cparams("parallel", "arbitrary"),
        name="outproj_ln1",
    )(a1, a2, w, res, g.reshape(1, -1), b.reshape(1, -1))


def _memattn_kernel(x_ref, mk_ref, mv_ref, wq_ref, wo_ref, g_ref, b_ref, o_ref):
    x = x_ref[...]
    q = jnp.dot(x.astype(BF16), wq_ref[...], preferred_element_type=F32) * (MEM_DH ** -0.5)
    mk = mk_ref[...].astype(BF16)
    mv = mv_ref[...].astype(BF16)
    heads = []
    for h in range(MEM_HEADS):
        sl = slice(h * MEM_DH, (h + 1) * MEM_DH)
        s = lax.dot_general(q[:, sl].astype(BF16), mk[:, sl], _NT, preferred_element_type=F32)
        p = jnp.exp(s - jnp.max(s, axis=-1, keepdims=True))
        l = jnp.sum(p, axis=-1, keepdims=True)
        heads.append(jnp.dot(p.astype(BF16), mv[:, sl], preferred_element_type=F32) / l)
    o = jnp.concatenate(heads, axis=1).astype(BF16)
    y = ALPHA * x + jnp.dot(o, wo_ref[...], preferred_element_type=F32)
    o_ref[...] = _layernorm(y, g_ref[...], b_ref[...])


def _memattn_ln(x, mk, mv, wq, wo, g, b, n_batch, tm):
    m, d = x.shape
    t = m // n_batch
    assert t % tm == 0
    nt = t // tm
    return pl.pallas_call(
        _memattn_kernel,
        grid=(n_batch, nt),
        in_specs=[pl.BlockSpec((tm, d), lambda bi, i: (bi * nt + i, 0)),
                  pl.BlockSpec((None, MEM_LEN, MEM_WIDTH), lambda bi, i: (bi, 0, 0)),
                  pl.BlockSpec((None, MEM_LEN, MEM_WIDTH), lambda bi, i: (bi, 0, 0)),
                  pl.BlockSpec((d, MEM_WIDTH), lambda bi, i: (0, 0)),
                  pl.BlockSpec((MEM_WIDTH, d), lambda bi, i: (0, 0)),
                  pl.BlockSpec((1, d), lambda bi, i: (0, 0)),
                  pl.BlockSpec((1, d), lambda bi, i: (0, 0))],
        out_specs=pl.BlockSpec((tm, d), lambda bi, i: (bi * nt + i, 0)),
        out_shape=jax.ShapeDtypeStruct((m, d), F32),
        compiler_params=_cparams("parallel", "arbitrary"),
        name="memattn_ln2",
    )(x, mk, mv, wq, wo, g.reshape(1, -1), b.reshape(1, -1))


def _swiglu_kernel(x_ref, wg_ref, wu_ref, wd_ref, g_ref, b_ref, o_ref, xb_scr):
    f = pl.program_id(1)
    nf = pl.num_programs(1)

    @pl.when(f == 0)
    def _init():
        x = x_ref[...]
        xb_scr[...] = x.astype(BF16)
        o_ref[...] = ALPHA * x

    xb = xb_scr[...]
    gate = jnp.dot(xb, wg_ref[...], preferred_element_type=F32)
    up = jnp.dot(xb, wu_ref[...], preferred_element_type=F32)
    hidden = (gate * _sigmoid(gate) * up).astype(BF16)
    o_ref[...] += jnp.dot(hidden, wd_ref[...], preferred_element_type=F32)

    @pl.when(f == nf - 1)
    def _finish():
        o_ref[...] = _layernorm(o_ref[...], g_ref[...], b_ref[...])


def _swiglu_ln(x, wg, wu, wd, g, b, tm, tf):
    m, d = x.shape
    f = wg.shape[1]
    assert m % tm == 0 and f % tf == 0
    return pl.pallas_call(
        _swiglu_kernel,
        grid=(m // tm, f // tf),
        in_specs=[pl.BlockSpec((tm, d), lambda i, j: (i, 0), pipeline_mode=pl.Buffered(1)),
                  pl.BlockSpec((d, tf), lambda i, j: (0, j)),
                  pl.BlockSpec((d, tf), lambda i, j: (0, j)),
                  pl.BlockSpec((tf, d), lambda i, j: (j, 0)),
                  pl.BlockSpec((1, d), lambda i, j: (0, 0)),
                  pl.BlockSpec((1, d), lambda i, j: (0, 0))],
        out_specs=pl.BlockSpec((tm, d), lambda i, j: (i, 0), pipeline_mode=pl.Buffered(1)),
        out_shape=jax.ShapeDtypeStruct((m, d), F32),
        scratch_shapes=[pltpu.VMEM((tm, d), BF16)],
        compiler_params=_cparams("parallel", "arbitrary"),
        name="swiglu_ln3",
    )(x, wg, wu, wd, g.reshape(1, -1), b.reshape(1, -1))


def _pad_feat(x):
    c = 3 * RW_WIDTH
    z = jnp.zeros(x.shape[:-1] + (LORA_PAD - DECAY_LORA,), x.dtype)
    return jnp.concatenate([x[..., :c + DECAY_LORA], z, x[..., c + DECAY_LORA:c + DECAY_LORA + AAA_LORA], z,
                            x[..., c + DECAY_LORA + AAA_LORA:]], axis=-1)


def _unpad_feat(x):
    c = 3 * RW_WIDTH
    return jnp.concatenate([x[..., :c + DECAY_LORA], x[..., c + LORA_PAD:c + LORA_PAD + AAA_LORA],
                            x[..., c + 2 * LORA_PAD:]], axis=-1)


def _pad_rows(w, rows):
    return jnp.concatenate([w, jnp.zeros((rows - w.shape[0],) + w.shape[1:], w.dtype)], axis=0)


def _layer(x2d, n_batch, attn_fn, wkv0, shift0_pad, mem_k, mem_v, p, cfg):
    xb = x2d.astype(BF16)
    mm = lambda col0, ncols: _matmul(xb, p["w_in"], cfg["mm_tm"], cfg["mm_tn"], col0, ncols)
    q = mm(0, DA_WIDTH)
    k = mm(DA_WIDTH, DA_WIDTH)
    v = mm(2 * DA_WIDTH, DA_WIDTH)
    feat = mm(3 * DA_WIDTH, 3 * RW_WIDTH)
    lora = _matmul(xb, p["w_in_lora"], cfg["mm_tm"], LORA_BLOCK)
    t = x2d.shape[0] // n_batch
    last = jnp.concatenate([feat.reshape(n_batch, t, -1)[:, t - 1], lora.reshape(n_batch, t, -1)[:, t - 1]], -1)
    o_a = attn_fn(q, k, v)
    feats = _rwkv_prep(feat, lora, shift0_pad, p["mu_shift"], p["w0"], p["a0"], p["k_k"], p["k_a"],
                       p["w2"], p["a2"], p["g2"], n_batch, cfg["prep_tm"])
    if cfg["wkv_chunk"]:
        o_r, y_t = _wkv_chunked(feats, _state_to_blockdiag(wkv0), p["lnx_g"], p["lnx_b"], p["r_k"], n_batch,
                                cfg["wkv_chunk"], cfg["wkv_tiles"])
        wkv_t = _blockdiag_to_state(y_t)
    else:
        o_r, s_t = _wkv(feats, _state_to_tiles(wkv0), p["lnx_g"], p["lnx_b"], p["r_k"], n_batch, cfg["wkv_tb"])
        wkv_t = _tiles_to_state(s_t)
    x1 = _outproj_ln(o_a, o_r, p["w_out"], x2d, p["ln1_g"], p["ln1_b"], cfg["op_tm"], cfg["op_tn"])
    x2 = _memattn_ln(x1, mem_k, mem_v, p["wq_m"], p["wo_m"], p["ln2_g"], p["ln2_b"], n_batch, cfg["ma_tm"])
    y = _swiglu_ln(x2, p["w_gate"], p["w_up"], p["w_down"], p["ln3_g"], p["ln3_b"], cfg["ff_tm"], cfg["ff_tf"])
    return y, k, v, last, wkv_t


_PROMPT_CFG = dict(mm_tm=1024, mm_tn=512, prep_tm=256, wkv_chunk=64, wkv_tiles=8, op_tm=512, op_tn=512,
                   ma_tm=256, ff_tm=512, ff_tf=256, attn_tq=1024, attn_rc=512)
_SAMPLE_CFG = dict(mm_tm=256, mm_tn=512, prep_tm=8, wkv_chunk=0, wkv_tb=8, op_tm=256, op_tn=512, ma_tm=8,
                   ff_tm=256, ff_tf=256, pages_per_step=8)


def kernel(x_prompt, x_sample, cache_k, cache_v, cache_mem_k, cache_mem_v, state_wkv, state_shift, page_table, mem_prompt, w_in, lam_q1, lam_k1, lam_q2, lam_k2, subln_g, mu_shift, w0, w2, a0, a2, g2, k_k, k_a, r_k, lnx_g, lnx_b, w_out, ln1_g, ln1_b, wq_m, wk_m, wv_m, wo_m, ln2_g, ln2_b, w_gate, w_up, w_down, ln3_g, ln3_b):
    bp, tp, d = x_prompt.shape
    bs, ts, _ = x_sample.shape
    yp = x_prompt.reshape(bp * tp, d)
    ys = x_sample.reshape(bs * ts, d)
    outs = [[] for _ in range(10)]
    for l in range(DEPTH):
        lam_init = 0.8 - 0.6 * math.exp(-0.3 * l)
        c = 3 * DA_WIDTH + 3 * RW_WIDTH
        w_in_b = w_in[l].astype(BF16)
        zc = jnp.zeros((d, LORA_PAD - DECAY_LORA), BF16)
        w_lora = w_in_b[:, c:]
        w_in_lora = jnp.concatenate(
            [w_lora[:, :DECAY_LORA], zc, w_lora[:, DECAY_LORA:DECAY_LORA + AAA_LORA], zc,
             w_lora[:, DECAY_LORA + AAA_LORA:]], axis=1)
        p = dict(
            w_in=w_in_b, w_in_lora=w_in_lora, mu_shift=_pad_feat(mu_shift[l]), w0=w0[l], a0=a0[l], k_k=k_k[l], k_a=k_a[l],
            w2=_pad_rows(w2[l], LORA_PAD).astype(BF16), a2=_pad_rows(a2[l], LORA_PAD).astype(BF16),
            g2=g2[l].astype(BF16), r_k=r_k[l], lnx_g=lnx_g[l], lnx_b=lnx_b[l],
            w_out=w_out[l].astype(BF16), ln1_g=ln1_g[l], ln1_b=ln1_b[l],
            wq_m=wq_m[l].astype(BF16), wo_m=wo_m[l].astype(BF16), ln2_g=ln2_g[l], ln2_b=ln2_b[l],
            w_gate=w_gate[l].astype(BF16), w_up=w_up[l].astype(BF16), w_down=w_down[l].astype(BF16),
            ln3_g=ln3_g[l], ln3_b=ln3_b[l])
        lamv = jnp.stack([lam_q1[l], lam_k1[l], lam_q2[l], lam_k2[l]])
        g_sub = subln_g[l].reshape(1, DA_HV)

        w_kv = jnp.concatenate([wk_m[l], wv_m[l]], axis=1).astype(BF16)
        mem_kv = _matmul(mem_prompt.reshape(bp * MEM_LEN, d).astype(BF16), w_kv, MEM_LEN, MEM_WIDTH)
        mk_p = mem_kv[:, :MEM_WIDTH].reshape(bp, MEM_LEN, MEM_WIDTH)
        mv_p = mem_kv[:, MEM_WIDTH:].reshape(bp, MEM_LEN, MEM_WIDTH)
        attn_p = functools.partial(_prompt_attention, lamv=lamv, subln_g=g_sub, lam_init=lam_init,
                                   tq=_PROMPT_CFG["attn_tq"], rc=_PROMPT_CFG["attn_rc"])
        yp, k_p, v_p, last_p, wkv_p = _layer(yp, bp, attn_p, jnp.zeros((bp, RW_HEADS, RW_HEAD, RW_HEAD), F32),
                                   jnp.zeros((bp, FEAT_PAD), F32), mk_p, mv_p, p, _PROMPT_CFG)

        def attn_s(q, k, v, l=l, lamv=lamv, g_sub=g_sub, lam_init=lam_init):
            o = _sample_attention(q, k, v, cache_k[l], cache_v[l], page_table, lamv, g_sub, lam_init, ts,
                                  _SAMPLE_CFG["pages_per_step"])
            return o.astype(BF16)

        ys, k_s, v_s, last_s, wkv_s = _layer(ys, bs, attn_s, state_wkv[l], _pad_feat(state_shift[l]),
                                   cache_mem_k[l].reshape(bs, MEM_LEN, MEM_WIDTH),
                                   cache_mem_v[l].reshape(bs, MEM_LEN, MEM_WIDTH), p, _SAMPLE_CFG)

        outs[0].append(k_p.reshape(bp, tp, DA_HEADS, DA_HV))
        outs[1].append(v_p.reshape(bp, tp, DA_HEADS, DA_HV))
        outs[2].append(wkv_p)
        outs[3].append(_unpad_feat(last_p))
        outs[4].append(mk_p.reshape(bp, MEM_LEN, MEM_HEADS, MEM_DH))
        outs[5].append(mv_p.reshape(bp, MEM_LEN, MEM_HEADS, MEM_DH))
        outs[6].append(k_s.reshape(bs, ts, DA_HEADS, DA_HV))
        outs[7].append(v_s.reshape(bs, ts, DA_HEADS, DA_HV))
        outs[8].append(wkv_s)
        outs[9].append(_unpad_feat(last_s))
    return (yp.reshape(bp, tp, d), ys.reshape(bs, ts, d)) + tuple(jnp.stack(o) for o in outs)
```

```python
import functools
import math

import jax
import jax.numpy as jnp
import numpy as np
from jax import lax
from jax.experimental import pallas as pl
from jax.experimental.pallas import tpu as pltpu

F32 = jnp.float32
BF16 = jnp.bfloat16

D_MODEL = 4096
DA_HEADS = 8
DA_DH = 128
DA_HV = 2 * DA_DH
DA_WIDTH = DA_HEADS * DA_HV
RW_HEAD = 64
RW_WIDTH = 2048
RW_HEADS = RW_WIDTH // RW_HEAD
DECAY_LORA = 96
AAA_LORA = 96
GATE_LORA = 256
RW_PROJ = 3 * RW_WIDTH + DECAY_LORA + AAA_LORA + GATE_LORA
LORA_PAD = 128
LORA_BLOCK = 2 * LORA_PAD + GATE_LORA
FEAT_PAD = 3 * RW_WIDTH + LORA_BLOCK
MEM_LEN = 256
MEM_HEADS = 4
MEM_DH = 128
MEM_WIDTH = MEM_HEADS * MEM_DH
PAGE_SIZE = 128
DEPTH = 1
LN_EPS = 1e-5
SUBLN_EPS = 1e-5
RW_GN_EPS = 64e-5
ALPHA = (2.0 * DEPTH) ** 0.25

LANES = 128
VMEM_LIMIT = 56 * 1024 * 1024
NEG_INF = float("-inf")

_NT = (((1,), (1,)), ((), ()))
LOG2E = math.log2(math.e)
QK_SCALE_LOG2 = DA_DH ** -0.5 * LOG2E


def _cparams(*sem):
    return pltpu.CompilerParams(dimension_semantics=sem, vmem_limit_bytes=VMEM_LIMIT)


def _layernorm(x, g, b):
    mu = jnp.mean(x, axis=-1, keepdims=True)
    d = x - mu
    var = jnp.mean(d * d, axis=-1, keepdims=True)
    return d * lax.rsqrt(var + LN_EPS) * g + b


def _sigmoid(x):
    return 1.0 / (1.0 + jnp.exp(-x))


def _split2(x):
    hi = x.astype(BF16)
    lo = (x - hi.astype(F32)).astype(BF16)
    return hi, lo


def _segsum(x, jj_ref):
    hi, lo = _split2(x)
    return jnp.dot(jnp.concatenate([hi, lo], axis=1), jj_ref[...], preferred_element_type=F32)


def _seg_ones(parts):
    r = np.arange(parts * LANES)[:, None] % LANES
    c = np.arange(LANES)[None, :]
    return jnp.asarray((r // RW_HEAD) == (c // RW_HEAD), dtype=BF16)


def _diag_mask():
    r = np.arange(RW_HEAD)[:, None]
    c = np.arange(LANES)[None, :]
    return jnp.asarray((c % RW_HEAD) == r, dtype=F32)


def _mm_kernel(a_ref, w_ref, o_ref):
    o_ref[...] = jnp.dot(a_ref[...], w_ref[...], preferred_element_type=F32)


def _matmul(a, w, tm, tn, col0=0, ncols=None):
    m, k = a.shape
    n = w.shape[1] if ncols is None else ncols
    assert m % tm == 0 and n % tn == 0 and col0 % tn == 0
    j0 = col0 // tn
    return pl.pallas_call(
        _mm_kernel,
        grid=(m // tm, n // tn),
        in_specs=[pl.BlockSpec((tm, k), lambda i, j: (i, 0)),
                  pl.BlockSpec((k, tn), lambda i, j: (0, j0 + j))],
        out_specs=pl.BlockSpec((tm, tn), lambda i, j: (i, j)),
        out_shape=jax.ShapeDtypeStruct((m, n), F32),
        compiler_params=_cparams("parallel", "arbitrary"),
        name="matmul",
    )(a, w)


def _lambda(lamv_ref, lam_init):
    lv = lamv_ref[...]
    s1 = jnp.sum(lv[0:1] * lv[1:2], axis=-1, keepdims=True)
    s2 = jnp.sum(lv[2:3] * lv[3:4], axis=-1, keepdims=True)
    return jnp.exp(s1) - jnp.exp(s2) + lam_init


def _diff_combine(acc_ref, l_ref, i1, i2, lam, g, lam_init):
    o = acc_ref[i1] / l_ref[i1] - lam * (acc_ref[i2] / l_ref[i2])
    o = o * lax.rsqrt(jnp.mean(o * o, axis=-1, keepdims=True) + SUBLN_EPS) * g
    return o * (1.0 - lam_init)


def _pattn_kernel(qi_ref, ki_ref, slopes_ref, lamv_ref, g_ref, q_ref, k_ref, v_ref, o_ref,
                  qs_scr, m_scr, l_scr, acc_scr, *, lam_init, tq, rc):
    h = pl.program_id(0)
    st = pl.program_id(1)
    qb = qi_ref[st]
    kb = ki_ref[st]
    slope = slopes_ref[h]

    @pl.when(kb == 0)
    def _init():
        qs_scr[...] = (q_ref[...] * QK_SCALE_LOG2).astype(BF16)
        m_scr[...] = jnp.full(m_scr.shape, NEG_INF, F32)
        l_scr[...] = jnp.zeros(l_scr.shape, F32)
        acc_scr[...] = jnp.zeros(acc_scr.shape, F32)

    def _step(masked):
        k = k_ref[...].astype(BF16)
        v = v_ref[...].astype(BF16)
        col = lax.broadcasted_iota(jnp.int32, (1, tq), 1)
        bias = slope * ((kb - qb) * tq + col).astype(F32)
        pieces = [(c, r0) for r0 in range(0, tq, rc) for c in range(2)]

        def scores(c, r0):
            s = lax.dot_general(qs_scr[r0:r0 + rc, c * DA_DH:(c + 1) * DA_DH], k[:, c * DA_DH:(c + 1) * DA_DH],
                                _NT, preferred_element_type=F32) + bias
            if masked:
                row2 = r0 + lax.broadcasted_iota(jnp.int32, (rc, tq), 0)
                col2 = lax.broadcasted_iota(jnp.int32, (rc, tq), 1)
                s = jnp.where(col2 > row2, NEG_INF, s)
            return s

        def softmax(c, r0, s):
            m_old = m_scr[c, r0:r0 + rc]
            m_new = jnp.maximum(m_old, jnp.max(s, axis=-1, keepdims=True))
            corr = jnp.exp2(m_old - m_new)
            p = jnp.exp2(s - m_new)
            l_scr[c, r0:r0 + rc] = l_scr[c, r0:r0 + rc] * corr + jnp.sum(p, axis=-1, keepdims=True)
            m_scr[c, r0:r0 + rc] = m_new
            return p.astype(BF16), corr

        def values(c, r0, p, corr):
            acc_scr[c, r0:r0 + rc] = acc_scr[c, r0:r0 + rc] * corr + jnp.dot(p, v, preferred_element_type=F32)

        n = len(pieces)
        s_q, p_q = {}, {}
        for i in range(n + 2):
            if i < n:
                s_q[i] = scores(*pieces[i])
            if 1 <= i <= n:
                p_q[i - 1] = softmax(*pieces[i - 1], s_q.pop(i - 1))
            if i >= 2:
                values(*pieces[i - 2], *p_q.pop(i - 2))

    @pl.when(kb < qb)
    def _off_diag():
        _step(False)

    @pl.when(kb == qb)
    def _diag():
        _step(True)
        lam = _lambda(lamv_ref, lam_init)
        o_ref[...] = _diff_combine(acc_scr, l_scr, 0, 1, lam, g_ref[...], lam_init).astype(o_ref.dtype)


def _prompt_attention(q, k, v, lamv, subln_g, lam_init, tq, rc):
    t = q.shape[0]
    assert t % tq == 0 and tq % rc == 0
    nq = t // tq
    qi = np.concatenate([np.full(q + 1, q, np.int32) for q in range(nq)])
    ki = np.concatenate([np.arange(q + 1, dtype=np.int32) for q in range(nq)])
    slopes = jnp.asarray(LOG2E * 2.0 ** (-8.0 * np.arange(1, DA_HEADS + 1) / DA_HEADS), F32)
    grid_spec = pltpu.PrefetchScalarGridSpec(
        num_scalar_prefetch=3,
        grid=(DA_HEADS, len(qi)),
        in_specs=[
            pl.BlockSpec((4, DA_DH), lambda h, s, qi, ki, sl: (0, 0)),
            pl.BlockSpec((1, DA_HV), lambda h, s, qi, ki, sl: (0, 0)),
            pl.BlockSpec((tq, DA_HV), lambda h, s, qi, ki, sl: (qi[s], h)),
            pl.BlockSpec((tq, DA_HV), lambda h, s, qi, ki, sl: (ki[s], h)),
            pl.BlockSpec((tq, DA_HV), lambda h, s, qi, ki, sl: (ki[s], h)),
        ],
        out_specs=pl.BlockSpec((tq, DA_HV), lambda h, s, qi, ki, sl: (qi[s], h)),
        scratch_shapes=[
            pltpu.VMEM((tq, DA_HV), BF16),
            pltpu.VMEM((2, tq, 1), F32),
            pltpu.VMEM((2, tq, 1), F32),
            pltpu.VMEM((2, tq, DA_HV), F32),
        ],
    )
    return pl.pallas_call(
        functools.partial(_pattn_kernel, lam_init=lam_init, tq=tq, rc=rc),
        grid_spec=grid_spec,
        out_shape=jax.ShapeDtypeStruct((t, DA_WIDTH), BF16),
        compiler_params=_cparams("parallel", "arbitrary"),
        name="prompt_attention",
    )(jnp.asarray(qi), jnp.asarray(ki), slopes, lamv, subln_g, q, k, v)


def _sattn_kernel(pt_ref, lamv_ref, g_ref, q_ref, kn_ref, vn_ref, *rest, pages_per_step, lam_init):
    pp = pages_per_step
    kp_refs = rest[:pp]
    vp_refs = rest[pp:2 * pp]
    o_ref = rest[2 * pp]
    qs_scr, bias_scr, m_scr, l_scr, acc_scr = rest[2 * pp + 1:]
    st = pl.program_id(1)
    n_steps = pl.num_programs(1)
    rows = q_ref.shape[0]
    cols = PAGE_SIZE * DA_HEADS
    past_len = n_steps * pp * PAGE_SIZE

    head_r = lax.broadcasted_iota(jnp.int32, (rows, 1), 0) % DA_HEADS
    slope = LOG2E * jnp.exp2((head_r + 1).astype(F32) * (-8.0 / DA_HEADS))

    @pl.when(st == 0)
    def _init():
        qs_scr[...] = (q_ref[...] * QK_SCALE_LOG2).astype(BF16)
        lane = lax.broadcasted_iota(jnp.int32, (rows, cols), 1)
        row = lax.broadcasted_iota(jnp.int32, (rows, cols), 0)
        same_head = (lane % DA_HEADS) == (row % DA_HEADS)
        bias_scr[...] = jnp.where(same_head, slope * (lane // DA_HEADS).astype(F32), NEG_INF)
        m_scr[...] = jnp.full(m_scr.shape, NEG_INF, F32)
        l_scr[...] = jnp.zeros(l_scr.shape, F32)
        acc_scr[...] = jnp.zeros(acc_scr.shape, F32)

    def _blocks(k_refs, v_refs, biases):
        ks = [r[...].astype(BF16) for r in k_refs]
        vs = [r[...].astype(BF16) for r in v_refs]
        ps, corrs = [], []
        for c in range(2):
            q = qs_scr[:, c * DA_DH:(c + 1) * DA_DH]
            ss = [lax.dot_general(q, k[:, c * DA_DH:(c + 1) * DA_DH], _NT, preferred_element_type=F32) + b
                  for k, b in zip(ks, biases)]
            m_old = m_scr[c]
            m_new = functools.reduce(jnp.maximum, [jnp.max(s, axis=-1, keepdims=True) for s in ss], m_old)
            corr = jnp.exp2(m_old - m_new)
            pc = [jnp.exp2(s - m_new) for s in ss]
            l_scr[c] = l_scr[c] * corr + functools.reduce(
                lambda x, y: x + y, [jnp.sum(p, axis=-1, keepdims=True) for p in pc])
            m_scr[c] = m_new
            ps.append([p.astype(BF16) for p in pc])
            corrs.append(corr)
        pv = functools.reduce(lambda x, y: x + y, [
            jnp.dot(jnp.concatenate([ps[0][i], ps[1][i]], axis=0), vs[i], preferred_element_type=F32)
            for i in range(len(vs))])
        for c in range(2):
            acc_scr[c] = acc_scr[c] * corrs[c] + pv[c * rows:(c + 1) * rows]

    base = bias_scr[...]
    _blocks(kp_refs, vp_refs,
            [base + slope * ((st * pp + p) * PAGE_SIZE - past_len).astype(F32) for p in range(pp)])

    @pl.when(st == n_steps - 1)
    def _last():
        lane = lax.broadcasted_iota(jnp.int32, (rows, cols), 1)
        row = lax.broadcasted_iota(jnp.int32, (rows, cols), 0)
        causal = (lane // DA_HEADS) <= (row // DA_HEADS)
        _blocks([kn_ref], [vn_ref], [jnp.where(causal, bias_scr[...], NEG_INF)])
        lam = _lambda(lamv_ref, lam_init)
        o_ref[...] = _diff_combine(acc_scr, l_scr, 0, 1, lam, g_ref[...], lam_init).astype(o_ref.dtype)


def _sample_attention(q2d, k2d, v2d, cache_k, cache_v, page_table, lamv, subln_g, lam_init, n_tok,
                      pages_per_step):
    b, n_pages = page_table.shape
    pp = pages_per_step
    assert n_pages % pp == 0
    n_phys = cache_k.shape[0]
    rows = n_tok * DA_HEADS
    cols = PAGE_SIZE * DA_HEADS
    ck = cache_k.reshape(n_phys, cols, DA_HV)
    cv = cache_v.reshape(n_phys, cols, DA_HV)
    q = q2d.reshape(b, rows, DA_HV)
    pad = ((0, 0), (0, PAGE_SIZE - n_tok), (0, 0))
    k_new = jnp.pad(k2d.reshape(b, n_tok, DA_WIDTH), pad).reshape(b, cols, DA_HV)
    v_new = jnp.pad(v2d.reshape(b, n_tok, DA_WIDTH), pad).reshape(b, cols, DA_HV)

    def page_spec(p):
        return pl.BlockSpec((None, cols, DA_HV), lambda bi, s, pt, p=p: (pt[bi * n_pages + s * pp + p], 0, 0))

    grid_spec = pltpu.PrefetchScalarGridSpec(
        num_scalar_prefetch=1,
        grid=(b, n_pages // pp),
        in_specs=[
            pl.BlockSpec((4, DA_DH), lambda bi, s, pt: (0, 0)),
            pl.BlockSpec((1, DA_HV), lambda bi, s, pt: (0, 0)),
            pl.BlockSpec((None, rows, DA_HV), lambda bi, s, pt: (bi, 0, 0)),
            pl.BlockSpec((None, cols, DA_HV), lambda bi, s, pt: (bi, 0, 0)),
            pl.BlockSpec((None, cols, DA_HV), lambda bi, s, pt: (bi, 0, 0)),
        ] + [page_spec(p) for p in range(pp)] + [page_spec(p) for p in range(pp)],
        out_specs=pl.BlockSpec((None, rows, DA_HV), lambda bi, s, pt: (bi, 0, 0)),
        scratch_shapes=[
            pltpu.VMEM((rows, DA_HV), BF16),
            pltpu.VMEM((rows, cols), F32),
            pltpu.VMEM((2, rows, 1), F32),
            pltpu.VMEM((2, rows, 1), F32),
            pltpu.VMEM((2, rows, DA_HV), F32),
        ],
    )
    out = pl.pallas_call(
        functools.partial(_sattn_kernel, pages_per_step=pp, lam_init=lam_init),
        grid_spec=grid_spec,
        out_shape=jax.ShapeDtypeStruct((b, rows, DA_HV), F32),
        compiler_params=_cparams("parallel", "arbitrary"),
        name="sample_attention",
    )(page_table.reshape(-1), lamv, subln_g, q, k_new, v_new, *([ck] * pp), *([cv] * pp))
    return out.reshape(b * n_tok, DA_WIDTH)


def _rwkv_prep_kernel(fr_ref, fk_ref, fv_ref, fl_ref, pr_ref, pk_ref, pv_ref, pl_ref,
                      mur_ref, muk_ref, muv_ref, mul_ref, w0_ref, a0_ref, kkw_ref, kaw_ref,
                      w2_ref, a2_ref, g2_ref, jj_ref,
                      r_out, w_out, k_out, v_out, kk_out, b_out, g_out,
                      cr_scr, ck_scr, cv_scr, cl_scr):
    tb = pl.program_id(1)
    tm = fr_ref.shape[0]

    @pl.when(tb == 0)
    def _load_state():
        cr_scr[...] = pr_ref[...]
        ck_scr[...] = pk_ref[...]
        cv_scr[...] = pv_ref[...]
        cl_scr[...] = pl_ref[...]

    def mixed(f_ref, carry_scr, mu_ref):
        x = f_ref[...]
        row = lax.broadcasted_iota(jnp.int32, x.shape, 0)
        shifted = jnp.where(row == 0, carry_scr[...], pltpu.roll(x, 1, 0))
        carry_scr[...] = x[tm - 1:tm, :]
        return x + (shifted - x) * mu_ref[...]

    xr = mixed(fr_ref, cr_scr, mur_ref)
    xk = mixed(fk_ref, ck_scr, muk_ref)
    xv = mixed(fv_ref, cv_scr, muv_ref)
    xl = mixed(fl_ref, cl_scr, mul_ref)

    dw = jnp.tanh(xl[:, 0:LORA_PAD]).astype(BF16)
    da = xl[:, LORA_PAD:2 * LORA_PAD].astype(BF16)
    dg = _sigmoid(xl[:, 2 * LORA_PAD:]).astype(BF16)
    wl = w0_ref[...] + jnp.dot(dw, w2_ref[...], preferred_element_type=F32)
    neg = -wl
    softplus = jnp.maximum(neg, 0.0) + jnp.log(1.0 + jnp.exp(-jnp.abs(neg)))
    log_w = -jnp.exp(-softplus - 0.5)
    a = _sigmoid(a0_ref[...] + jnp.dot(da, a2_ref[...], preferred_element_type=F32))
    g = jnp.dot(dg, g2_ref[...], preferred_element_type=F32)

    kk0 = xk * kkw_ref[...]
    sq = kk0 * kk0
    ss = jnp.concatenate([_segsum(sq[:, n * LANES:(n + 1) * LANES], jj_ref)
                          for n in range(RW_WIDTH // LANES)], axis=1)
    kk = kk0 / jnp.maximum(jnp.sqrt(ss), 1e-12)

    r_out[...] = xr
    w_out[...] = log_w
    k_out[...] = xk * (1.0 + (a - 1.0) * kaw_ref[...])
    v_out[...] = xv
    kk_out[...] = kk
    b_out[...] = kk * a
    g_out[...] = g


def _rwkv_prep(proj, lora, prev_pad, mu_pad, w0, a0, k_k, k_a, w2p, a2p, g2b, n_batch, tm):
    m = proj.shape[0]
    t = m // n_batch
    assert t % tm == 0
    nt = t // tm
    base = 0
    prev3 = prev_pad.reshape(n_batch, 1, FEAT_PAD)
    mu2 = mu_pad.reshape(1, FEAT_PAD)

    def feat(sec):
        return pl.BlockSpec((tm, RW_WIDTH), lambda b, i, sec=sec: (b * nt + i, base + sec))

    def prev(sec):
        return pl.BlockSpec((None, 1, RW_WIDTH), lambda b, i, sec=sec: (b, 0, sec))

    def mu(sec):
        return pl.BlockSpec((1, RW_WIDTH), lambda b, i, sec=sec: (0, sec))

    vec = pl.BlockSpec((1, RW_WIDTH), lambda b, i: (0, 0))
    out = pl.BlockSpec((tm, RW_WIDTH), lambda b, i: (b * nt + i, 0))
    in_specs = [
        feat(0), feat(1), feat(2),
        pl.BlockSpec((tm, LORA_BLOCK), lambda b, i: (b * nt + i, 0)),
        prev(0), prev(1), prev(2),
        pl.BlockSpec((None, 1, LORA_BLOCK), lambda b, i: (b, 0, 3 * RW_WIDTH // LORA_BLOCK)),
        mu(0), mu(1), mu(2),
        pl.BlockSpec((1, LORA_BLOCK), lambda b, i: (0, 3 * RW_WIDTH // LORA_BLOCK)),
        vec, vec, vec, vec,
        pl.BlockSpec((LORA_PAD, RW_WIDTH), lambda b, i: (0, 0)),
        pl.BlockSpec((LORA_PAD, RW_WIDTH), lambda b, i: (0, 0)),
        pl.BlockSpec((GATE_LORA, RW_WIDTH), lambda b, i: (0, 0)),
        pl.BlockSpec((2 * LANES, LANES), lambda b, i: (0, 0)),
    ]
    shp = jax.ShapeDtypeStruct((m, RW_WIDTH), F32)
    return pl.pallas_call(
        _rwkv_prep_kernel,
        grid=(n_batch, nt),
        in_specs=in_specs,
        out_specs=[out] * 7,
        out_shape=[shp] * 7,
        scratch_shapes=[pltpu.VMEM((1, RW_WIDTH), F32)] * 3 + [pltpu.VMEM((1, LORA_BLOCK), F32)],
        compiler_params=_cparams("parallel", "arbitrary"),
        name="rwkv_prep",
    )(proj, proj, proj, lora, prev3, prev3, prev3, prev3, mu2, mu2, mu2, mu2,
      w0.reshape(1, -1), a0.reshape(1, -1), k_k.reshape(1, -1), k_a.reshape(1, -1),
      w2p, a2p, g2b, _seg_ones(2))


N_TILES = RW_WIDTH // LANES
S_ROWS = N_TILES * RW_HEAD


def _wkv_kernel(r_ref, w_ref, k_ref, v_ref, kk_ref, b_ref, g_ref, s0_ref,
                lng_ref, lnb_ref, rk_ref, jj2_ref, jj3_ref, dm_ref,
                o_ref, st_ref, s_scr, o_scr):
    tb = pl.program_id(1)
    n_tb = pl.num_programs(1)
    nb = s0_ref.shape[0]
    t_blk = r_ref.shape[0] // nb
    seqs = range(nb)

    @pl.when(tb == 0)
    def _load_state():
        s_scr[...] = s0_ref[...]

    dm = dm_ref[...]
    dm_all = jnp.concatenate([dm] * N_TILES, axis=0)
    dm_all_b = dm_all.astype(BF16)

    def expand(row):
        return jnp.concatenate(
            [jnp.broadcast_to(row[:, n * LANES:(n + 1) * LANES], (RW_HEAD, LANES)) for n in range(N_TILES)],
            axis=0)

    def spread_v(v_row):
        v_hi = v_row.astype(BF16)
        rem = v_row - v_hi.astype(F32)
        v_mid = rem.astype(BF16)
        v_lo = (rem - v_mid.astype(F32)).astype(BF16)
        v_parts = jnp.concatenate([expand(p) * dm_all_b for p in (v_hi, v_mid, v_lo)], axis=1)
        return jnp.dot(v_parts, jj3_ref[...], preferred_element_type=F32)

    def step(t, carry):
        row = [pl.ds(i * t_blk + t, 1) for i in seqs]
        s = [s_scr[i] for i in seqs]
        sa = [-_segsum(s[i] * expand(kk_ref[row[i], :]), jj2_ref) for i in seqs]
        vx = [spread_v(v_ref[row[i], :]) for i in seqs]
        s_new = [s[i] * expand(jnp.exp(w_ref[row[i], :]))
                 + sa[i] * expand(b_ref[row[i], :]) + vx[i] * expand(k_ref[row[i], :]) for i in seqs]
        for i in seqs:
            s_scr[i] = s_new[i]
        ox = [_segsum(s_new[i] * expand(r_ref[row[i], :]), jj2_ref) * dm_all for i in seqs]
        for i in seqs:
            o_scr[row[i], :] = jnp.concatenate(
                [jnp.sum(ox[i][n * RW_HEAD:(n + 1) * RW_HEAD], axis=0, keepdims=True) for n in range(N_TILES)],
                axis=1)
        return carry

    lax.fori_loop(0, t_blk, step, 0)

    for n in range(N_TILES):
        sl = slice(n * LANES, (n + 1) * LANES)
        o = o_scr[:, sl]
        mu = _segsum(o, jj2_ref) * (1.0 / RW_HEAD)
        d = o - mu
        var = _segsum(d * d, jj2_ref) * (1.0 / RW_HEAD)
        on = d * lax.rsqrt(var + RW_GN_EPS) * lng_ref[:, sl] + lnb_ref[:, sl]
        bonus = _segsum(r_ref[:, sl] * k_ref[:, sl] * rk_ref[:, sl], jj2_ref) * v_ref[:, sl]
        o_ref[:, sl] = ((on + bonus) * g_ref[:, sl]).astype(o_ref.dtype)

    @pl.when(tb == n_tb - 1)
    def _store_state():
        st_ref[...] = s_scr[...]


def _state_to_tiles(s):
    b = s.shape[0]
    return s.reshape(b, N_TILES, 2, RW_HEAD, RW_HEAD).transpose(0, 1, 3, 2, 4).reshape(b, S_ROWS, LANES)


def _tiles_to_state(s):
    b = s.shape[0]
    return s.reshape(b, N_TILES, RW_HEAD, 2, RW_HEAD).transpose(0, 1, 3, 2, 4).reshape(
        b, RW_HEADS, RW_HEAD, RW_HEAD)


def _wkv(feats, s0_tiles, lnx_g, lnx_b, r_k, n_batch, t_blk, nb):
    r, w, k, v, kk, bb, g = feats
    m = r.shape[0]
    t = m // n_batch
    assert t % t_blk == 0 and n_batch % nb == 0
    nt = t // t_blk
    assert nb == 1 or nt == 1
    blk = pl.BlockSpec((nb * t_blk, RW_WIDTH), lambda b, i: (b * nt + i, 0))
    vec = pl.BlockSpec((1, RW_WIDTH), lambda b, i: (0, 0))
    st = pl.BlockSpec((nb, S_ROWS, LANES), lambda b, i: (b, 0, 0))
    return pl.pallas_call(
        _wkv_kernel,
        grid=(n_batch // nb, nt),
        in_specs=[blk] * 7 + [st, vec, vec, vec,
                              pl.BlockSpec((2 * LANES, LANES), lambda b, i: (0, 0)),
                              pl.BlockSpec((3 * LANES, LANES), lambda b, i: (0, 0)),
                              pl.BlockSpec((RW_HEAD, LANES), lambda b, i: (0, 0))],
        out_specs=[blk, st],
        out_shape=[jax.ShapeDtypeStruct((m, RW_WIDTH), BF16),
                   jax.ShapeDtypeStruct((n_batch, S_ROWS, LANES), F32)],
        scratch_shapes=[pltpu.VMEM((nb, S_ROWS, LANES), F32), pltpu.VMEM((nb * t_blk, RW_WIDTH), F32)],
        compiler_params=_cparams("parallel", "arbitrary"),
        name="wkv",
    )(r, w, k, v, kk, bb, g, s0_tiles, lnx_g.reshape(1, -1), lnx_b.reshape(1, -1), r_k.reshape(1, -1),
      _seg_ones(2), _seg_ones(3), _diag_mask())


_NN = (((1,), (0,)), ((), ()))


def _mm3(a, b, dims=_NN):
    zero = jnp.zeros_like(b[0])
    lhs = jnp.concatenate([a[0], a[1]], axis=1)
    if dims == _NN:
        rhs = jnp.concatenate([jnp.concatenate([b[0], b[1]], axis=1), jnp.concatenate([b[0], zero], axis=1)], axis=0)
    else:
        rhs = jnp.concatenate([jnp.concatenate([b[0], b[0]], axis=1), jnp.concatenate([b[1], zero], axis=1)], axis=0)
    out = lax.dot_general(lhs, rhs, dims, preferred_element_type=F32)
    n = out.shape[1] // 2
    return out[:, :n] + out[:, n:]


def _wkv_chunk_tiles(lw, r, k, v, kk, b, ys_in, c_len):
    rows = 2 * c_len
    lane = lax.broadcasted_iota(jnp.int32, (1, LANES), 1)
    head_a = (lane < RW_HEAD).astype(F32)
    head_b = 1.0 - head_a

    def stack(x):
        return jnp.concatenate([x * head_a, x * head_b], axis=0)

    def each(fn, *lists):
        return [fn(*args) for args in zip(*lists)]

    tr = lax.broadcasted_iota(jnp.int32, (c_len, c_len), 0)
    tc = lax.broadcasted_iota(jnp.int32, (c_len, c_len), 1)
    tril = (tc <= tr).astype(BF16)

    def cumsum(x):
        l1 = x.astype(BF16)
        rem = x - l1.astype(F32)
        l2 = rem.astype(BF16)
        l3 = (rem - l2.astype(F32)).astype(BF16)
        return (jnp.dot(tril, l1, preferred_element_type=F32) + jnp.dot(tril, l2, preferred_element_type=F32)
                + jnp.dot(tril, l3, preferred_element_type=F32))

    cs = each(cumsum, lw)
    cs_last = [c[c_len - 1:c_len, :] for c in cs]
    g_rem = each(lambda c, cl: jnp.exp(cl - c), cs, cs_last)
    g_inv = each(lambda c: jnp.exp(-c), cs)
    a2f = each(lambda x, c, l: stack(-x * jnp.exp(c - l)), kk, cs, lw)
    r2f = each(lambda x, c: stack(x * jnp.exp(c)), r, cs)
    v2f = each(stack, v)
    a2 = each(_split2, a2f)
    r2 = each(_split2, r2f)
    v2 = each(_split2, v2f)
    b2 = each(lambda x, g: _split2(stack(x * g)), b, g_inv)
    k2 = each(lambda x, g: _split2(stack(x * g)), k, g_inv)
    bg2 = each(lambda x, g: _split2(stack(x * g)), b, g_rem)
    kg2 = each(lambda x, g: _split2(stack(x * g)), k, g_rem)

    rr = lax.broadcasted_iota(jnp.int32, (rows, rows), 0)
    cc = lax.broadcasted_iota(jnp.int32, (rows, rows), 1)
    same = (rr // c_len) == (cc // c_len)
    strict = same & ((cc % c_len) < (rr % c_len))
    lower = same & ((cc % c_len) <= (rr % c_len))
    eye = jnp.where(rr == cc, 1.0, 0.0)

    ar2 = each(lambda x, y: (jnp.concatenate([x[0], y[0]], axis=0), jnp.concatenate([x[1], y[1]], axis=0)), a2, r2)
    gb = each(lambda x, y: _mm3(x, y, _NT), ar2, b2)
    gk = each(lambda x, y: _mm3(x, y, _NT), ar2, k2)
    l_mat = [jnp.where(strict, x[:rows], 0.0) for x in gb]
    ak = [jnp.where(strict, x[:rows], 0.0) for x in gk]
    rb = [jnp.where(lower, x[rows:], 0.0) for x in gb]
    rk = [jnp.where(lower, x[rows:], 0.0) for x in gk]

    t_mat = [eye + x for x in l_mat]
    p_mat = l_mat
    for _ in range(max(int(math.log2(c_len)) - 1, 0)):
        ps = each(_split2, p_mat)
        p_mat = each(lambda x: _mm3(x, x), ps)
        t_mat = each(lambda t, p: t + _mm3(_split2(t), _split2(p)), t_mat, p_mat)

    akv = each(lambda x, y: _mm3(_split2(x), y), ak, v2)
    wu = each(lambda t, x, y: _mm3(_split2(t), _split2(jnp.concatenate([x, y], axis=1))),
              t_mat, a2f, akv)
    one = lambda x, y, dims=_NN: lax.dot_general(x.astype(BF16), y.astype(BF16), dims, preferred_element_type=F32)
    qo = each(one, rb, wu)
    q2 = each(lambda x, y: x + y[:, :LANES], r2f, qo)
    o0 = each(lambda q, x, y: q[:, LANES:] + one(x, y), qo, rk, v2f)
    mn = each(lambda x, y: _mm3(_split2(x.T), y), wu, bg2)
    n_mat = each(lambda m, x, y: m[LANES:] + _mm3(_split2(x.T), y), mn, v2f, kg2)

    ys = each(_split2, ys_in)
    o2 = each(lambda q, y, o: one(q, y, _NT) + o, q2, ys_in, o0)
    o = [x[:c_len] + x[c_len:] for x in o2]
    y_new = each(lambda y, cl, ysp, m, nm: y * jnp.exp(cl) + _mm3(ysp, _split2(m[:LANES])) + nm,
                 ys_in, cs_last, ys, mn, n_mat)
    return o, y_new


def _wkv_chunk_kernel(lw_ref, r_ref, k_ref, v_ref, kk_ref, b_ref, g_ref, y0_ref,
                      lng_ref, lnb_ref, rk_ref, jj2_ref, o_ref, yt_ref, y_scr):
    c = pl.program_id(2)
    n_c = pl.num_programs(2)
    c_len = r_ref.shape[0]
    tiles = r_ref.shape[1] // LANES

    @pl.when(c == 0)
    def _load_state():
        y_scr[...] = y0_ref[...]

    sls = [slice(t * LANES, (t + 1) * LANES) for t in range(tiles)]
    split = lambda ref: [ref[:, sl] for sl in sls]
    r, k, v = split(r_ref), split(k_ref), split(v_ref)
    o, y_new = _wkv_chunk_tiles(split(lw_ref), r, k, v, split(kk_ref), split(b_ref),
                                [y_scr[t] for t in range(tiles)], c_len)
    for t, sl in enumerate(sls):
        y_scr[t] = y_new[t]
        mu = _segsum(o[t], jj2_ref) * (1.0 / RW_HEAD)
        d = o[t] - mu
        var = _segsum(d * d, jj2_ref) * (1.0 / RW_HEAD)
        on = d * lax.rsqrt(var + RW_GN_EPS) * lng_ref[:, sl] + lnb_ref[:, sl]
        bonus = _segsum(r[t] * k[t] * rk_ref[:, sl], jj2_ref) * v[t]
        o_ref[:, sl] = ((on + bonus) * g_ref[:, sl]).astype(o_ref.dtype)

    @pl.when(c == n_c - 1)
    def _store_state():
        yt_ref[...] = y_scr[...]


def _state_to_blockdiag(s):
    b = s.shape[0]
    s5 = s.reshape(b, N_TILES, 2, RW_HEAD, RW_HEAD)
    z = jnp.zeros((b, N_TILES, RW_HEAD, RW_HEAD), s.dtype)
    top = jnp.concatenate([s5[:, :, 0], z], axis=-1)
    bot = jnp.concatenate([z, s5[:, :, 1]], axis=-1)
    return jnp.concatenate([top, bot], axis=-2)


def _blockdiag_to_state(y):
    b = y.shape[0]
    s = jnp.stack([y[:, :, :RW_HEAD, :RW_HEAD], y[:, :, RW_HEAD:, RW_HEAD:]], axis=2)
    return s.reshape(b, RW_HEADS, RW_HEAD, RW_HEAD)


def _wkv_chunked(feats, y0, lnx_g, lnx_b, r_k, n_batch, c_len, tiles):
    r, lw, k, v, kk, bb, g = feats
    m = r.shape[0]
    t = m // n_batch
    assert t % c_len == 0 and N_TILES % tiles == 0
    nc = t // c_len
    wid = tiles * LANES
    blk = pl.BlockSpec((c_len, wid), lambda b, gi, c: (b * nc + c, gi))
    vec = pl.BlockSpec((1, wid), lambda b, gi, c: (0, gi))
    st = pl.BlockSpec((None, tiles, LANES, LANES), lambda b, gi, c: (b, gi, 0, 0))
    return pl.pallas_call(
        _wkv_chunk_kernel,
        grid=(n_batch, N_TILES // tiles, nc),
        in_specs=[blk] * 7 + [st, vec, vec, vec, pl.BlockSpec((2 * LANES, LANES), lambda b, gi, c: (0, 0))],
        out_specs=[blk, st],
        out_shape=[jax.ShapeDtypeStruct((m, RW_WIDTH), BF16),
                   jax.ShapeDtypeStruct((n_batch, N_TILES, LANES, LANES), F32)],
        scratch_shapes=[pltpu.VMEM((tiles, LANES, LANES), F32)],
        compiler_params=_cparams("parallel", "parallel", "arbitrary"),
        name="wkv_chunked",
    )(lw, r, k, v, kk, bb, g, y0, lnx_g.reshape(1, -1), lnx_b.reshape(1, -1), r_k.reshape(1, -1), _seg_ones(2))


def _outproj_ln_kernel(a1_ref, a2_ref, w_ref, res_ref, g_ref, b_ref, o_ref, acc_scr):
    j = pl.program_id(1)
    nn = pl.num_programs(1)
    k1 = a1_ref.shape[1]
    mix = (jnp.dot(a1_ref[...], w_ref[:k1, :], preferred_element_type=F32)
           + jnp.dot(a2_ref[...], w_ref[k1:, :], preferred_element_type=F32))
    acc_scr[j] = ALPHA * res_ref[...] + mix

    @pl.when(j == nn - 1)
    def _finish():
        y = jnp.concatenate([acc_scr[n] for n in range(acc_scr.shape[0])], axis=1)
        o_ref[...] = _layernorm(y, g_ref[...], b_ref[...])


def _outproj_ln(a1, a2, w, res, g, b, tm, tn):
    m, k1 = a1.shape
    k2 = a2.shape[1]
    n = w.shape[1]
    assert m % tm == 0 and n % tn == 0
    return pl.pallas_call(
        _outproj_ln_kernel,
        grid=(m // tm, n // tn),
        in_specs=[pl.BlockSpec((tm, k1), lambda i, j: (i, 0)),
                  pl.BlockSpec((tm, k2), lambda i, j: (i, 0)),
                  pl.BlockSpec((k1 + k2, tn), lambda i, j: (0, j)),
                  pl.BlockSpec((tm, tn), lambda i, j: (i, j)),
                  pl.BlockSpec((1, n), lambda i, j: (0, 0)),
                  pl.BlockSpec((1, n), lambda i, j: (0, 0))],
        out_specs=pl.BlockSpec((tm, n), lambda i, j: (i, 0)),
        out_shape=jax.ShapeDtypeStruct((m, n), F32),
        scratch_shapes=[pltpu.VMEM((n // tn, tm, tn), F32)],
        compiler_params=_cparams("parallel", "arbitrary"),
        name="outproj_ln1",
    )(a1, a2, w, res, g.reshape(1, -1), b.reshape(1, -1))


def _memattn_kernel(x_ref, mk_ref, mv_ref, wq_ref, wo_ref, g_ref, b_ref, o_ref):
    x = x_ref[...]
    q = jnp.dot(x.astype(BF16), wq_ref[...], preferred_element_type=F32) * (MEM_DH ** -0.5)
    mk = mk_ref[...].astype(BF16)
    mv = mv_ref[...].astype(BF16)
    heads = []
    for h in range(MEM_HEADS):
        sl = slice(h * MEM_DH, (h + 1) * MEM_DH)
        s = lax.dot_general(q[:, sl].astype(BF16), mk[:, sl], _NT, preferred_element_type=F32)
        p = jnp.exp(s - jnp.max(s, axis=-1, keepdims=True))
        l = jnp.sum(p, axis=-1, keepdims=True)
        heads.append(jnp.dot(p.astype(BF16), mv[:, sl], preferred_element_type=F32) / l)
    o = jnp.concatenate(heads, axis=1).astype(BF16)
    y = ALPHA * x + jnp.dot(o, wo_ref[...], preferred_element_type=F32)
    o_ref[...] = _layernorm(y, g_ref[...], b_ref[...])


def _memattn_ln(x, mk, mv, wq, wo, g, b, n_batch, tm):
    m, d = x.shape
    t = m // n_batch
    assert t % tm == 0
    nt = t // tm
    return pl.pallas_call(
        _memattn_kernel,
        grid=(n_batch, nt),
        in_specs=[pl.BlockSpec((tm, d), lambda bi, i: (bi * nt + i, 0)),
                  pl.BlockSpec((None, MEM_LEN, MEM_WIDTH), lambda bi, i: (bi, 0, 0)),
                  pl.BlockSpec((None, MEM_LEN, MEM_WIDTH), lambda bi, i: (bi, 0, 0)),
                  pl.BlockSpec((d, MEM_WIDTH), lambda bi, i: (0, 0)),
                  pl.BlockSpec((MEM_WIDTH, d), lambda bi, i: (0, 0)),
                  pl.BlockSpec((1, d), lambda bi, i: (0, 0)),
                  pl.BlockSpec((1, d), lambda bi, i: (0, 0))],
        out_specs=pl.BlockSpec((tm, d), lambda bi, i: (bi * nt + i, 0)),
        out_shape=jax.ShapeDtypeStruct((m, d), F32),
        compiler_params=_cparams("parallel", "arbitrary"),
        name="memattn_ln2",
    )(x, mk, mv, wq, wo, g.reshape(1, -1), b.reshape(1, -1))


def _swiglu_kernel(x_ref, wg_ref, wu_ref, wd_ref, g_ref, b_ref, o_ref, xb_scr):
    f = pl.program_id(1)
    nf = pl.num_programs(1)

    @pl.when(f == 0)
    def _init():
        x = x_ref[...]
        xb_scr[...] = x.astype(BF16)
        o_ref[...] = ALPHA * x

    xb = xb_scr[...]
    gate = jnp.dot(xb, wg_ref[...], preferred_element_type=F32)
    up = jnp.dot(xb, wu_ref[...], preferred_element_type=F32)
    hidden = (gate * _sigmoid(gate) * up).astype(BF16)
    o_ref[...] += jnp.dot(hidden, wd_ref[...], preferred_element_type=F32)

    @pl.when(f == nf - 1)
    def _finish():
        o_ref[...] = _layernorm(o_ref[...], g_ref[...], b_ref[...])


def _swiglu_ln(x, wg, wu, wd, g, b, tm, tf):
    m, d = x.shape
    f = wg.shape[1]
    assert m % tm == 0 and f % tf == 0
    return pl.pallas_call(
        _swiglu_kernel,
        grid=(m // tm, f // tf),
        in_specs=[pl.BlockSpec((tm, d), lambda i, j: (i, 0), pipeline_mode=pl.Buffered(1)),
                  pl.BlockSpec((d, tf), lambda i, j: (0, j)),
                  pl.BlockSpec((d, tf), lambda i, j: (0, j)),
                  pl.BlockSpec((tf, d), lambda i, j: (j, 0)),
                  pl.BlockSpec((1, d), lambda i, j: (0, 0)),
                  pl.BlockSpec((1, d), lambda i, j: (0, 0))],
        out_specs=pl.BlockSpec((tm, d), lambda i, j: (i, 0), pipeline_mode=pl.Buffered(1)),
        out_shape=jax.ShapeDtypeStruct((m, d), F32),
        scratch_shapes=[pltpu.VMEM((tm, d), BF16)],
        compiler_params=_cparams("parallel", "arbitrary"),
        name="swiglu_ln3",
    )(x, wg, wu, wd, g.reshape(1, -1), b.reshape(1, -1))


def _pad_feat(x):
    c = 3 * RW_WIDTH
    z = jnp.zeros(x.shape[:-1] + (LORA_PAD - DECAY_LORA,), x.dtype)
    return jnp.concatenate([x[..., :c + DECAY_LORA], z, x[..., c + DECAY_LORA:c + DECAY_LORA + AAA_LORA], z,
                            x[..., c + DECAY_LORA + AAA_LORA:]], axis=-1)


def _unpad_feat(x):
    c = 3 * RW_WIDTH
    return jnp.concatenate([x[..., :c + DECAY_LORA], x[..., c + LORA_PAD:c + LORA_PAD + AAA_LORA],
                            x[..., c + 2 * LORA_PAD:]], axis=-1)


def _pad_rows(w, rows):
    return jnp.concatenate([w, jnp.zeros((rows - w.shape[0],) + w.shape[1:], w.dtype)], axis=0)


def _layer(x2d, n_batch, attn_fn, wkv0, shift0_pad, mem_k, mem_v, p, cfg):
    xb = x2d.astype(BF16)
    mm = lambda col0, ncols: _matmul(xb, p["w_in"], cfg["mm_tm"], cfg["mm_tn"], col0, ncols)
    q = mm(0, DA_WIDTH)
    k = mm(DA_WIDTH, DA_WIDTH)
    v = mm(2 * DA_WIDTH, DA_WIDTH)
    feat = mm(3 * DA_WIDTH, 3 * RW_WIDTH)
    lora = _matmul(xb, p["w_in_lora"], cfg["mm_tm"], LORA_BLOCK)
    t = x2d.shape[0] // n_batch
    last = jnp.concatenate([feat.reshape(n_batch, t, -1)[:, t - 1], lora.reshape(n_batch, t, -1)[:, t - 1]], -1)
    o_a = attn_fn(q, k, v)
    feats = _rwkv_prep(feat, lora, shift0_pad, p["mu_shift"], p["w0"], p["a0"], p["k_k"], p["k_a"],
                       p["w2"], p["a2"], p["g2"], n_batch, cfg["prep_tm"])
    if cfg["wkv_chunk"]:
        o_r, y_t = _wkv_chunked(feats, _state_to_blockdiag(wkv0), p["lnx_g"], p["lnx_b"], p["r_k"], n_batch,
                                cfg["wkv_chunk"], cfg["wkv_tiles"])
        wkv_t = _blockdiag_to_state(y_t)
    else:
        o_r, s_t = _wkv(feats, _state_to_tiles(wkv0), p["lnx_g"], p["lnx_b"], p["r_k"], n_batch, cfg["wkv_tb"],
                        cfg["wkv_nb"])
        wkv_t = _tiles_to_state(s_t)
    x1 = _outproj_ln(o_a, o_r, p["w_out"], x2d, p["ln1_g"], p["ln1_b"], cfg["op_tm"], cfg["op_tn"])
    x2 = _memattn_ln(x1, mem_k, mem_v, p["wq_m"], p["wo_m"], p["ln2_g"], p["ln2_b"], n_batch, cfg["ma_tm"])
    y = _swiglu_ln(x2, p["w_gate"], p["w_up"], p["w_down"], p["ln3_g"], p["ln3_b"], cfg["ff_tm"], cfg["ff_tf"])
    return y, k, v, last, wkv_t


_PROMPT_CFG = dict(mm_tm=1024, mm_tn=512, prep_tm=256, wkv_chunk=64, wkv_tiles=8, op_tm=512, op_tn=512,
                   ma_tm=256, ff_tm=512, ff_tf=256, attn_tq=1024, attn_rc=512)
_SAMPLE_CFG = dict(mm_tm=256, mm_tn=512, prep_tm=8, wkv_chunk=0, wkv_tb=8, wkv_nb=4, op_tm=256, op_tn=512, ma_tm=8,
                   ff_tm=256, ff_tf=256, pages_per_step=8)


def kernel(x_prompt, x_sample, cache_k, cache_v, cache_mem_k, cache_mem_v, state_wkv, state_shift, page_table, mem_prompt, w_in, lam_q1, lam_k1, lam_q2, lam_k2, subln_g, mu_shift, w0, w2, a0, a2, g2, k_k, k_a, r_k, lnx_g, lnx_b, w_out, ln1_g, ln1_b, wq_m, wk_m, wv_m, wo_m, ln2_g, ln2_b, w_gate, w_up, w_down, ln3_g, ln3_b):
    bp, tp, d = x_prompt.shape
    bs, ts, _ = x_sample.shape
    yp = x_prompt.reshape(bp * tp, d)
    ys = x_sample.reshape(bs * ts, d)
    outs = [[] for _ in range(10)]
    for l in range(DEPTH):
        lam_init = 0.8 - 0.6 * math.exp(-0.3 * l)
        c = 3 * DA_WIDTH + 3 * RW_WIDTH
        w_in_b = w_in[l].astype(BF16)
        zc = jnp.zeros((d, LORA_PAD - DECAY_LORA), BF16)
        w_lora = w_in_b[:, c:]
        w_in_lora = jnp.concatenate(
            [w_lora[:, :DECAY_LORA], zc, w_lora[:, DECAY_LORA:DECAY_LORA + AAA_LORA], zc,
             w_lora[:, DECAY_LORA + AAA_LORA:]], axis=1)
        p = dict(
            w_in=w_in_b, w_in_lora=w_in_lora, mu_shift=_pad_feat(mu_shift[l]), w0=w0[l], a0=a0[l], k_k=k_k[l], k_a=k_a[l],
            w2=_pad_rows(w2[l], LORA_PAD).astype(BF16), a2=_pad_rows(a2[l], LORA_PAD).astype(BF16),
            g2=g2[l].astype(BF16), r_k=r_k[l], lnx_g=lnx_g[l], lnx_b=lnx_b[l],
            w_out=w_out[l].astype(BF16), ln1_g=ln1_g[l], ln1_b=ln1_b[l],
            wq_m=wq_m[l].astype(BF16), wo_m=wo_m[l].astype(BF16), ln2_g=ln2_g[l], ln2_b=ln2_b[l],
            w_gate=w_gate[l].astype(BF16), w_up=w_up[l].astype(BF16), w_down=w_down[l].astype(BF16),
            ln3_g=ln3_g[l], ln3_b=ln3_b[l])
        lamv = jnp.stack([lam_q1[l], lam_k1[l], lam_q2[l], lam_k2[l]])
        g_sub = subln_g[l].reshape(1, DA_HV)

        w_kv = jnp.concatenate([wk_m[l], wv_m[l]], axis=1).astype(BF16)
        mem_kv = _matmul(mem_prompt.reshape(bp * MEM_LEN, d).astype(BF16), w_kv, MEM_LEN, MEM_WIDTH)
        mk_p = mem_kv[:, :MEM_WIDTH].reshape(bp, MEM_LEN, MEM_WIDTH)
        mv_p = mem_kv[:, MEM_WIDTH:].reshape(bp, MEM_LEN, MEM_WIDTH)
        attn_p = functools.partial(_prompt_attention, lamv=lamv, subln_g=g_sub, lam_init=lam_init,
                                   tq=_PROMPT_CFG["attn_tq"], rc=_PROMPT_CFG["attn_rc"])
        yp, k_p, v_p, last_p, wkv_p = _layer(yp, bp, attn_p, jnp.zeros((bp, RW_HEADS, RW_HEAD, RW_HEAD), F32),
                                   jnp.zeros((bp, FEAT_PAD), F32), mk_p, mv_p, p, _PROMPT_CFG)

        def attn_s(q, k, v, l=l, lamv=lamv, g_sub=g_sub, lam_init=lam_init):
            o = _sample_attention(q, k, v, cache_k[l], cache_v[l], page_table, lamv, g_sub, lam_init, ts,
                                  _SAMPLE_CFG["pages_per_step"])
            return o.astype(BF16)

        ys, k_s, v_s, last_s, wkv_s = _layer(ys, bs, attn_s, state_wkv[l], _pad_feat(state_shift[l]),
                                   cache_mem_k[l].reshape(bs, MEM_LEN, MEM_WIDTH),
                                   cache_mem_v[l].reshape(bs, MEM_LEN, MEM_WIDTH), p, _SAMPLE_CFG)

        outs[0].append(k_p.reshape(bp, tp, DA_HEADS, DA_HV))
        outs[1].append(v_p.reshape(bp, tp, DA_HEADS, DA_HV))
        outs[2].append(wkv_p)
        outs[3].append(_unpad_feat(last_p))
        outs[4].append(mk_p.reshape(bp, MEM_LEN, MEM_HEADS, MEM_DH))
        outs[5].append(mv_p.reshape(bp, MEM_LEN, MEM_HEADS, MEM_DH))
        outs[6].append(k_s.reshape(bs, ts, DA_HEADS, DA_HV))
        outs[7].append(v_s.reshape(bs, ts, DA_HEADS, DA_HV))
        outs[8].append(wkv_s)
        outs[9].append(_unpad_feat(last_s))
    return (yp.reshape(bp, tp, d), ys.reshape(bs, ts, d)) + tuple(jnp.stack(o) for o in outs)
```
